```python
import math
import jax, jax.numpy as jnp
from jax import lax
import numpy as np

D_MODEL = 2048
BATCH = 2
SEQ = 4096
DEPTH = 1
DEC_BATCH = 128
DEC_SEQ = 1
PAST_LEN = 16384
PAGE_SIZE = 128

HEAD_DIM = 64
N_HEADS = 16
N_KV_HEADS = 4
GQA_GROUP = N_HEADS // N_KV_HEADS
WINDOW = 128
ATTN_WIDTH = N_HEADS * HEAD_DIM
KV_WIDTH = N_KV_HEADS * HEAD_DIM
CONV_CH = D_MODEL // 2
CONV_WIDTH = 31
N_EXPERTS = 256
TOP_K = 8
N_EXPERT_GROUPS = 8
TOPK_GROUPS = 4
EXPERT_FF = D_MODEL // 4
SHARED_FF = EXPERT_FF
ROUTED_SCALE = 2.5
MOE_MAX_BLOCK = 128
LN_EPS = 1e-5
DEEPNORM_ALPHA = (2 * DEPTH) ** 0.25
DEEPNORM_BETA = (8 * DEPTH) ** -0.25
IN_SPLITS = [ATTN_WIDTH, ATTN_WIDTH + KV_WIDTH, ATTN_WIDTH + 2 * KV_WIDTH,
             ATTN_WIDTH + 2 * KV_WIDTH + CONV_CH, ATTN_WIDTH + 2 * KV_WIDTH + 2 * CONV_CH,
             ATTN_WIDTH + 2 * KV_WIDTH + 2 * CONV_CH + D_MODEL]
IN_WIDTH = ATTN_WIDTH + 2 * KV_WIDTH + 2 * CONV_CH + 2 * D_MODEL

kernel_name = 'hybrid_swa_conformer_moe_deepnorm_step'


def layer_norm(x, g, b):
    x32 = x.astype(jnp.float32)
    mu = jnp.mean(x32, axis=-1, keepdims=True)
    var = jnp.mean(jnp.square(x32 - mu), axis=-1, keepdims=True)
    y = (x32 - mu) * lax.rsqrt(var + LN_EPS) * g.astype(jnp.float32) + b.astype(jnp.float32)
    return y.astype(x.dtype)


def alibi_slopes():
    return 2.0 ** (-8.0 * (jnp.arange(N_HEADS, dtype=jnp.float32) + 1.0) / N_HEADS)


def window_attention(q, k, v, q_pos, k_pos, sinks):
    s = jnp.einsum('...qkgd,...skd->...kgqs', q, k).astype(jnp.float32) * (HEAD_DIM ** -0.5)
    dist = q_pos[..., :, None] - k_pos[..., None, :]
    valid = (dist >= 0) & (dist <= WINDOW) & (k_pos[..., None, :] >= 0)
    slopes = alibi_slopes().reshape(N_KV_HEADS, GQA_GROUP, 1, 1)
    s = s - slopes * dist[..., None, None, :, :].astype(jnp.float32)
    s = jnp.where(valid[..., None, None, :, :], s, -jnp.inf)
    sink = jnp.broadcast_to(sinks.astype(jnp.float32).reshape(N_KV_HEADS, GQA_GROUP, 1, 1), s.shape[:-1] + (1,))
    p = jax.nn.softmax(jnp.concatenate([s, sink], axis=-1), axis=-1)[..., :-1]
    return jnp.einsum('...kgqs,...skd->...qkgd', p.astype(v.dtype), v)


def causal_depthwise_conv(u_ext, w, b):
    y = lax.conv_general_dilated(u_ext, w[:, None, :].astype(u_ext.dtype), window_strides=(1,), padding='VALID',
                                 dimension_numbers=('NWC', 'WIO', 'NWC'), feature_group_count=CONV_CH)
    return y + b.astype(y.dtype)


def swiglu(h, wg, wu, wd):
    return (jax.nn.silu(h @ wg) * (h @ wu)) @ wd


def moe_block_rows(n_assign):
    per_expert = max(1, -(-n_assign // N_EXPERTS))
    rows = 8
    while rows < per_expert and rows < MOE_MAX_BLOCK:
        rows *= 2
    return rows


def routed_experts(h, idx, gate_w, w_gate_e, w_up_e, w_down_e):
    T, D = h.shape
    A = T * TOP_K
    block = moe_block_rows(A)
    n_blocks = -(-(A + N_EXPERTS * (block - 1)) // block)
    P = n_blocks * block
    expert_of = idx.reshape(-1)
    token_of = jnp.repeat(jnp.arange(T, dtype=jnp.int32), TOP_K)
    weight_of = gate_w.reshape(-1)
    order = jnp.argsort(expert_of, stable=True)
    e_sorted = expert_of[order]
    counts = jnp.bincount(expert_of, length=N_EXPERTS)
    starts = jnp.cumsum(counts) - counts
    padded = ((counts + block - 1) // block) * block
    pends = jnp.cumsum(padded)
    pstarts = pends - padded
    dest = pstarts[e_sorted] + (jnp.arange(A) - starts[e_sorted])
    buf_tok = jnp.full((P,), T, jnp.int32).at[dest].set(token_of[order])
    buf_w = jnp.zeros((P,), jnp.float32).at[dest].set(weight_of[order])
    block_start = jnp.arange(n_blocks) * block
    block_exp = jnp.minimum(jnp.sum(pends[None, :] <= block_start[:, None], axis=1), N_EXPERTS - 1)
    h_pad = jnp.concatenate([h, jnp.zeros((1, D), h.dtype)], axis=0)

    def step(acc, blk):
        tok, wt, e = blk
        xb = h_pad[tok]
        y = swiglu(xb, w_gate_e[e], w_up_e[e], w_down_e[e])
        return acc.at[tok].add(y * wt[:, None].astype(y.dtype)), None

    acc, _ = lax.scan(step, jnp.zeros((T + 1, D), h.dtype),
                      (buf_tok.reshape(n_blocks, block), buf_w.reshape(n_blocks, block), block_exp))
    return acc[:T]


def moe_ffn(h, p):
    T = h.shape[0]
    scores = jax.nn.sigmoid(h.astype(jnp.float32) @ p['w_router'].astype(jnp.float32))
    sel = scores + p['b_router'].astype(jnp.float32)
    grp = sel.reshape(T, N_EXPERT_GROUPS, N_EXPERTS // N_EXPERT_GROUPS)
    grp_score = lax.top_k(grp, 2)[0].sum(-1)
    _, top_g = lax.top_k(grp_score, TOPK_GROUPS)
    gmask = jnp.any(top_g[:, :, None] == jnp.arange(N_EXPERT_GROUPS)[None, None, :], axis=1)
    sel = jnp.where(jnp.repeat(gmask, N_EXPERTS // N_EXPERT_GROUPS, axis=1), sel, -jnp.inf)
    _, idx = lax.top_k(sel, TOP_K)
    w = jnp.take_along_axis(scores, idx, axis=-1)
    w = w / jnp.sum(w, axis=-1, keepdims=True) * ROUTED_SCALE
    routed = routed_experts(h, idx, w, p['w_gate_e'], p['w_up_e'], p['w_down_e'])
    shared = swiglu(h, p['w_gate_s'], p['w_up_s'], p['w_down_s'])
    return routed + shared


def decoder_layer(x, past_k, past_v, past_conv, p, decode):
    B, L, _ = x.shape
    z = x @ p['w_in']
    q, k, v, ca, cb, ga, gc = jnp.split(z, IN_SPLITS, axis=-1)
    q = q.reshape(B, L, N_KV_HEADS, GQA_GROUP, HEAD_DIM)
    k = k.reshape(B, L, N_KV_HEADS, HEAD_DIM)
    v = v.reshape(B, L, N_KV_HEADS, HEAD_DIM)
    u = ca * jax.nn.sigmoid(cb)
    if decode:
        k_all = jnp.concatenate([past_k.astype(k.dtype), k], axis=1)
        v_all = jnp.concatenate([past_v.astype(v.dtype), v], axis=1)
        q_pos = PAST_LEN + jnp.arange(L)
        k_pos = PAST_LEN - WINDOW + jnp.arange(WINDOW + L)
        o = window_attention(q, k_all, v_all, q_pos, k_pos, p['sinks'])
        new_k, new_v = k_all[:, -WINDOW:], v_all[:, -WINDOW:]
        u_ext = jnp.concatenate([past_conv.astype(u.dtype), u], axis=1)
    else:
        nb = L // WINDOW
        qb = q.reshape(B, nb, WINDOW, N_KV_HEADS, GQA_GROUP, HEAD_DIM)
        kb = k.reshape(B, nb, WINDOW, N_KV_HEADS, HEAD_DIM)
        vb = v.reshape(B, nb, WINDOW, N_KV_HEADS, HEAD_DIM)
        shift = ((0, 0), (1, 0), (0, 0), (0, 0), (0, 0))
        k_band = jnp.concatenate([jnp.pad(kb, shift)[:, :-1], kb], axis=2)
        v_band = jnp.concatenate([jnp.pad(vb, shift)[:, :-1], vb], axis=2)
        q_pos = jnp.arange(L).reshape(nb, WINDOW)
        k_pos = q_pos[:, :1] - WINDOW + jnp.arange(2 * WINDOW)[None, :]
        o = window_attention(qb, k_band, v_band, q_pos, k_pos, p['sinks'])
        new_k, new_v = k[:, -WINDOW:], v[:, -WINDOW:]
        u_ext = jnp.pad(u, ((0, 0), (CONV_WIDTH - 1, 0), (0, 0)))
    new_conv = u_ext[:, -(CONV_WIDTH - 1):]
    o = o.reshape(B, L, ATTN_WIDTH)
    c = causal_depthwise_conv(u_ext, p['conv_w'], p['conv_b'])
    c = jax.nn.silu(layer_norm(c, p['conv_ln_g'], p['conv_ln_b']))
    mixed = jax.nn.sigmoid(ga) * (o @ p['w_attn_proj']) + jax.nn.sigmoid(gc) * (c @ p['w_conv_proj'])
    x = layer_norm(DEEPNORM_ALPHA * x + mixed @ p['w_out'], p['ln1_g'], p['ln1_b'])
    f = moe_ffn(x.reshape(B * L, D_MODEL), p).reshape(B, L, D_MODEL)
    x = layer_norm(DEEPNORM_ALPHA * x + f, p['ln2_g'], p['ln2_b'])
    return x, new_k, new_v, new_conv


def setup_inputs(seed: int = 0) -> dict:
    key = jax.random.key(seed)
    ks = jax.random.split(key, 32)
    nrm = lambda i, shape, s: jax.random.normal(ks[i], shape, jnp.float32) * s
    Ld = DEPTH
    return {
        'x_prompt': nrm(0, (BATCH, SEQ, D_MODEL), 1.0),
        'x_sample': nrm(1, (DEC_BATCH, DEC_SEQ, D_MODEL), 1.0),
        'cache_k': nrm(2, (Ld, DEC_BATCH, WINDOW, N_KV_HEADS, HEAD_DIM), 1.0),
        'cache_v': nrm(3, (Ld, DEC_BATCH, WINDOW, N_KV_HEADS, HEAD_DIM), 1.0),
        'state_conv': nrm(4, (Ld, DEC_BATCH, CONV_WIDTH - 1, CONV_CH), 0.5),
        'w_in': nrm(5, (Ld, D_MODEL, IN_WIDTH), D_MODEL ** -0.5),
        'sinks': nrm(6, (Ld, N_HEADS), 1.0),
        'w_attn_proj': nrm(7, (Ld, ATTN_WIDTH, D_MODEL), ATTN_WIDTH ** -0.5),
        'conv_w': nrm(8, (Ld, CONV_WIDTH, CONV_CH), CONV_WIDTH ** -0.5),
        'conv_b': nrm(9, (Ld, CONV_CH), 0.02),
        'conv_ln_g': 1.0 + nrm(10, (Ld, CONV_CH), 0.02),
        'conv_ln_b': nrm(11, (Ld, CONV_CH), 0.02),
        'w_conv_proj': nrm(12, (Ld, CONV_CH, D_MODEL), CONV_CH ** -0.5),
        'w_out': nrm(13, (Ld, D_MODEL, D_MODEL), D_MODEL ** -0.5 * DEEPNORM_BETA),
        'ln1_g': 1.0 + nrm(14, (Ld, D_MODEL), 0.02),
        'ln1_b': nrm(15, (Ld, D_MODEL), 0.02),
        'w_router': nrm(16, (Ld, D_MODEL, N_EXPERTS), D_MODEL ** -0.5),
        'b_router': nrm(17, (Ld, N_EXPERTS), 0.01),
        'w_gate_e': nrm(18, (Ld, N_EXPERTS, D_MODEL, EXPERT_FF), D_MODEL ** -0.5),
        'w_up_e': nrm(19, (Ld, N_EXPERTS, D_MODEL, EXPERT_FF), D_MODEL ** -0.5),
        'w_down_e': nrm(20, (Ld, N_EXPERTS, EXPERT_FF, D_MODEL), EXPERT_FF ** -0.5 * DEEPNORM_BETA),
        'w_gate_s': nrm(21, (Ld, D_MODEL, SHARED_FF), D_MODEL ** -0.5),
        'w_up_s': nrm(22, (Ld, D_MODEL, SHARED_FF), D_MODEL ** -0.5),
        'w_down_s': nrm(23, (Ld, SHARED_FF, D_MODEL), SHARED_FF ** -0.5 * DEEPNORM_BETA),
        'ln2_g': 1.0 + nrm(24, (Ld, D_MODEL), 0.02),
        'ln2_b': nrm(25, (Ld, D_MODEL), 0.02),
    }


def reference(x_prompt, x_sample, cache_k, cache_v, state_conv, w_in, sinks, w_attn_proj, conv_w, conv_b,
              conv_ln_g, conv_ln_b, w_conv_proj, w_out, ln1_g, ln1_b, w_router, b_router, w_gate_e, w_up_e,
              w_down_e, w_gate_s, w_up_s, w_down_s, ln2_g, ln2_b):
    yp, ys = x_prompt, x_sample
    kp, vp, cp, ksm, vsm, csm = [], [], [], [], [], []
    for l in range(DEPTH):
        p = {'w_in': w_in[l], 'sinks': sinks[l], 'w_attn_proj': w_attn_proj[l], 'conv_w': conv_w[l],
             'conv_b': conv_b[l], 'conv_ln_g': conv_ln_g[l], 'conv_ln_b': conv_ln_b[l],
             'w_conv_proj': w_conv_proj[l], 'w_out': w_out[l], 'ln1_g': ln1_g[l], 'ln1_b': ln1_b[l],
             'w_router': w_router[l], 'b_router': b_router[l], 'w_gate_e': w_gate_e[l], 'w_up_e': w_up_e[l],
             'w_down_e': w_down_e[l], 'w_gate_s': w_gate_s[l], 'w_up_s': w_up_s[l], 'w_down_s': w_down_s[l],
             'ln2_g': ln2_g[l], 'ln2_b': ln2_b[l]}
        yp, k1, v1, c1 = decoder_layer(yp, None, None, None, p, decode=False)
        ys, k2, v2, c2 = decoder_layer(ys, cache_k[l], cache_v[l], state_conv[l], p, decode=True)
        kp.append(k1); vp.append(v1); cp.append(c1)
        ksm.append(k2); vsm.append(v2); csm.append(c2)
    return (yp, ys, jnp.stack(kp), jnp.stack(vp), jnp.stack(cp), jnp.stack(ksm), jnp.stack(vsm), jnp.stack(csm))
```

```python
import functools

import jax
import jax.numpy as jnp
from jax import lax
from jax.experimental import pallas as pl
from jax.experimental.pallas import tpu as pltpu

HEAD_DIM = 64
N_HEADS = 16
N_KV_HEADS = 4
GQA_GROUP = N_HEADS // N_KV_HEADS
WINDOW = 128
ATTN_WIDTH = N_HEADS * HEAD_DIM
KV_WIDTH = N_KV_HEADS * HEAD_DIM
CONV_WIDTH = 31
N_EXPERTS = 256
TOP_K = 8
N_EXPERT_GROUPS = 8
GROUP_SIZE = N_EXPERTS // N_EXPERT_GROUPS
TOPK_GROUPS = 4
ROUTED_SCALE = 2.5
LN_EPS = 1e-5
DEPTH = 1
DEEPNORM_ALPHA = (2 * DEPTH) ** 0.25
ALIBI_SLOPES = tuple(2.0 ** (-8.0 * (h + 1.0) / N_HEADS) for h in range(N_HEADS))
ATTN_SCALE = HEAD_DIM ** -0.5

MOE_BLOCK = 128
TOKEN_TILE = 128
LANES = 128
SUBLANES = 8
VMEM_LIMIT = 56 * 1024 * 1024

BF16 = jnp.bfloat16
F32 = jnp.float32
NEG_INF = float("-inf")


def _cparams(*sem):
    return pltpu.CompilerParams(dimension_semantics=sem, vmem_limit_bytes=VMEM_LIMIT)


def _sigmoid(x):
    return 1.0 / (1.0 + jnp.exp(-x))


def _silu(x):
    return x * _sigmoid(x)


def _layer_norm(x, g, b):
    mu = jnp.mean(x, axis=-1, keepdims=True)
    xc = x - mu
    var = jnp.mean(xc * xc, axis=-1, keepdims=True)
    return xc * lax.rsqrt(var + LN_EPS) * g + b


def _inproj_kernel(x_ref, w_ref, o_ref, xb_ref):
    @pl.when(pl.program_id(1) == 0)
    def _():
        xb_ref[...] = x_ref[...].astype(BF16)

    o_ref[...] = jnp.dot(xb_ref[...], w_ref[...], preferred_element_type=F32)


def _inproj(x2d, w_bf, tm, tn):
    m, k = x2d.shape
    n = w_bf.shape[1]
    return pl.pallas_call(
        _inproj_kernel,
        grid=(m // tm, n // tn),
        in_specs=[pl.BlockSpec((tm, k), lambda i, j: (i, 0)),
                  pl.BlockSpec((k, tn), lambda i, j: (0, j))],
        out_specs=pl.BlockSpec((tm, tn), lambda i, j: (i, j)),
        out_shape=jax.ShapeDtypeStruct((m, n), F32),
        scratch_shapes=[pltpu.VMEM((tm, k), BF16)],
        compiler_params=_cparams("parallel", "arbitrary"),
        name="inproj",
    )(x2d, w_bf)


def _attn_prompt_kernel(sinks_ref, q_ref, kp_ref, kc_ref, vp_ref, vc_ref, o_ref):
    i = pl.program_id(1)
    q = q_ref[...]
    k = jnp.concatenate([kp_ref[...], kc_ref[...]], axis=0).astype(BF16)
    v = jnp.concatenate([vp_ref[...], vc_ref[...]], axis=0).astype(BF16)
    qi = lax.broadcasted_iota(jnp.int32, (WINDOW, 2 * WINDOW), 0)
    kj = lax.broadcasted_iota(jnp.int32, (WINDOW, 2 * WINDOW), 1)
    dist = qi + WINDOW - kj
    valid = (dist >= 0) & (dist <= WINDOW) & ((kj >= WINDOW) | (i > 0))
    distf = dist.astype(F32)
    outs = []
    for h in range(N_HEADS):
        kv = h // GQA_GROUP
        qh = q[:, h * HEAD_DIM:(h + 1) * HEAD_DIM].astype(BF16)
        kh = k[:, kv * HEAD_DIM:(kv + 1) * HEAD_DIM]
        vh = v[:, kv * HEAD_DIM:(kv + 1) * HEAD_DIM]
        s = lax.dot_general(qh, kh, (((1,), (1,)), ((), ())), preferred_element_type=F32) * ATTN_SCALE
        s = jnp.where(valid, s - ALIBI_SLOPES[h] * distf, NEG_INF)
        sink = sinks_ref[h]
        m = jnp.maximum(jnp.max(s, axis=-1, keepdims=True), sink)
        e = jnp.exp(s - m)
        denom = jnp.sum(e, axis=-1, keepdims=True) + jnp.exp(sink - m)
        p = (e / denom).astype(BF16)
        outs.append(jnp.dot(p, vh, preferred_element_type=F32))
    o_ref[...] = jnp.concatenate(outs, axis=-1).astype(o_ref.dtype)


def _attn_prompt(z2d, sinks, batch, seq):
    nb = seq // WINDOW
    kcol = ATTN_WIDTH // KV_WIDTH
    vcol = kcol + 1

    def cur(col):
        return lambda b, i: (b * nb + i, col)

    def prev(col):
        return lambda b, i: (b * nb + jnp.maximum(i - 1, 0), col)

    return pl.pallas_call(
        _attn_prompt_kernel,
        grid=(batch, nb),
        in_specs=[pl.BlockSpec(memory_space=pltpu.SMEM),
                  pl.BlockSpec((WINDOW, ATTN_WIDTH), cur(0)),
                  pl.BlockSpec((WINDOW, KV_WIDTH), prev(kcol)),
                  pl.BlockSpec((WINDOW, KV_WIDTH), cur(kcol)),
                  pl.BlockSpec((WINDOW, KV_WIDTH), prev(vcol)),
                  pl.BlockSpec((WINDOW, KV_WIDTH), cur(vcol))],
        out_specs=pl.BlockSpec((WINDOW, ATTN_WIDTH), cur(0)),
        out_shape=jax.ShapeDtypeStruct((batch * seq, ATTN_WIDTH), BF16),
        compiler_params=_cparams("parallel", "parallel"),
        name="attn_prompt",
    )(sinks, z2d, z2d, z2d, z2d, z2d)


def _attn_sample_kernel(sinks_ref, slopes_ref, q_ref, kn_ref, vn_ref, ck_ref, cv_ref,
                        o_ref, nk_ref, nv_ref, *, bb):
    dist_c = (WINDOW - lax.broadcasted_iota(jnp.int32, (1, WINDOW), 1)).astype(F32)
    for b in range(bb):
        q = q_ref[b]
        kc = ck_ref[b]
        vc = cv_ref[b]
        kn = kn_ref[b]
        vn = vn_ref[b]
        for kv in range(N_KV_HEADS):
            rows = slice(kv * GQA_GROUP, (kv + 1) * GQA_GROUP)
            cols = slice(kv * HEAD_DIM, (kv + 1) * HEAD_DIM)
            qh = q[rows, :].astype(BF16)
            sink = sinks_ref[rows, :]
            slope = slopes_ref[rows, :]
            sc = lax.dot_general(qh, kc[:, cols].astype(BF16), (((1,), (1,)), ((), ())),
                                 preferred_element_type=F32) * ATTN_SCALE - slope * dist_c
            knh = kn[:, cols].astype(BF16).astype(F32)
            sn = jnp.sum(qh.astype(F32) * knh, axis=-1, keepdims=True) * ATTN_SCALE
            m = jnp.maximum(jnp.maximum(jnp.max(sc, axis=-1, keepdims=True), sn), sink)
            ec = jnp.exp(sc - m)
            en = jnp.exp(sn - m)
            denom = jnp.sum(ec, axis=-1, keepdims=True) + en + jnp.exp(sink - m)
            pc = (ec / denom).astype(BF16)
            pn = (en / denom).astype(BF16).astype(F32)
            vnh = vn[:, cols].astype(BF16).astype(F32)
            oh = jnp.dot(pc, vc[:, cols].astype(BF16), preferred_element_type=F32) + pn * vnh
            o_ref[b, rows, :] = oh.astype(o_ref.dtype)
        nk_ref[b, 0:WINDOW - 1, :] = kc[1:WINDOW, :]
        nk_ref[b, WINDOW - 1:WINDOW, :] = kn
        nv_ref[b, 0:WINDOW - 1, :] = vc[1:WINDOW, :]
        nv_ref[b, WINDOW - 1:WINDOW, :] = vn


def _attn_sample(q3, kn3, vn3, ck, cv, sinks_col, slopes_col, bb):
    nbatch = q3.shape[0]
    blk3 = lambda s1, s2: pl.BlockSpec((bb, s1, s2), lambda i: (i, 0, 0))
    full = lambda a: pl.BlockSpec(a.shape, lambda i: (0,) * a.ndim)
    return pl.pallas_call(
        functools.partial(_attn_sample_kernel, bb=bb),
        grid=(nbatch // bb,),
        in_specs=[full(sinks_col), full(slopes_col),
                  blk3(N_HEADS, HEAD_DIM), blk3(1, KV_WIDTH), blk3(1, KV_WIDTH),
                  blk3(WINDOW, KV_WIDTH), blk3(WINDOW, KV_WIDTH)],
        out_specs=[blk3(N_HEADS, HEAD_DIM), blk3(WINDOW, KV_WIDTH), blk3(WINDOW, KV_WIDTH)],
        out_shape=[jax.ShapeDtypeStruct((nbatch, N_HEADS, HEAD_DIM), BF16),
                   jax.ShapeDtypeStruct((nbatch, WINDOW, KV_WIDTH), F32),
                   jax.ShapeDtypeStruct((nbatch, WINDOW, KV_WIDTH), F32)],
        compiler_params=_cparams("parallel"),
        name="attn_sample",
    )(sinks_col, slopes_col, q3, kn3, vn3, ck, cv)


CONV_HALO = 32
CONV_ROWS = 64


def _conv_prompt_kernel(ca0, ca1, cb0, cb1, w_ref, b_ref, g_ref, beta_ref,
                        c_ref, nc_ref, ext, ybuf, *, ts):
    i = pl.program_id(1)
    half = ca0.shape[1]

    @pl.when(i == 0)
    def _():
        ext[0:CONV_HALO, :] = jnp.zeros((CONV_HALO, ext.shape[1]), F32)

    @pl.when(i > 0)
    def _():
        ext[0:CONV_HALO, :] = ext[ts:ts + CONV_HALO, :]

    ext[CONV_HALO:CONV_HALO + ts, 0:half] = ca0[...] * _sigmoid(cb0[...])
    ext[CONV_HALO:CONV_HALO + ts, half:2 * half] = ca1[...] * _sigmoid(cb1[...])

    shift = CONV_HALO - (CONV_WIDTH - 1)
    for r in range(ts // CONV_ROWS):
        for c in range(ext.shape[1] // LANES):
            cols = slice(c * LANES, (c + 1) * LANES)
            acc = jnp.zeros((CONV_ROWS, LANES), F32)
            for j in range(CONV_WIDTH):
                r0 = r * CONV_ROWS + shift + j
                acc = acc + w_ref[j:j + 1, cols] * ext[r0:r0 + CONV_ROWS, cols]
            ybuf[r * CONV_ROWS:(r + 1) * CONV_ROWS, cols] = acc

    y = ybuf[...] + b_ref[...]
    c_ref[...] = _silu(_layer_norm(y, g_ref[...], beta_ref[...])).astype(c_ref.dtype)

    @pl.when(i == pl.num_programs(1) - 1)
    def _():
        nc_ref[...] = ext[CONV_HALO + ts - (CONV_WIDTH - 1):CONV_HALO + ts, :]


def _conv_prompt(z2d, conv_w, conv_b, ln_g, ln_b, batch, seq, ts):
    nt = seq // ts
    ch = conv_w.shape[1]
    half = ch // 2
    ca_col = (ATTN_WIDTH + 2 * KV_WIDTH) // half
    cb_col = ca_col + 2
    zblk = lambda col: pl.BlockSpec((ts, half), lambda b, i: (b * nt + i, col))
    full = lambda a: pl.BlockSpec(a.shape, lambda b, i: (0,) * a.ndim)
    return pl.pallas_call(
        functools.partial(_conv_prompt_kernel, ts=ts),
        grid=(batch, nt),
        in_specs=[zblk(ca_col), zblk(ca_col + 1), zblk(cb_col), zblk(cb_col + 1),
                  full(conv_w), full(conv_b), full(ln_g), full(ln_b)],
        out_specs=[pl.BlockSpec((ts, ch), lambda b, i: (b * nt + i, 0)),
                   pl.BlockSpec((None, CONV_WIDTH - 1, ch), lambda b, i: (b, 0, 0))],
        out_shape=[jax.ShapeDtypeStruct((batch * seq, ch), BF16),
                   jax.ShapeDtypeStruct((batch, CONV_WIDTH - 1, ch), F32)],
        scratch_shapes=[pltpu.VMEM((CONV_HALO + ts, ch), F32), pltpu.VMEM((ts, ch), F32)],
        compiler_params=_cparams("parallel", "arbitrary"),
        name="conv_prompt",
    )(z2d, z2d, z2d, z2d, conv_w, conv_b, ln_g, ln_b)


def _conv_sample_kernel(st_ref, ca_ref, cb_ref, w_ref, b_ref, g_ref, beta_ref, c_ref, ns_ref, *, bb):
    u = ca_ref[...] * _sigmoid(cb_ref[...])
    w_past = w_ref[0:CONV_WIDTH - 1, :]
    w_new = w_ref[CONV_WIDTH - 1:CONV_WIDTH, :]
    for b in range(bb):
        st = st_ref[b]
        ub = u[b:b + 1, :]
        y = jnp.sum(st * w_past, axis=0, keepdims=True) + ub * w_new + b_ref[...]
        c_ref[b:b + 1, :] = _silu(_layer_norm(y, g_ref[...], beta_ref[...])).astype(c_ref.dtype)
        ns_ref[b, 0:CONV_WIDTH - 2, :] = st[1:CONV_WIDTH - 1, :]
        ns_ref[b, CONV_WIDTH - 2:CONV_WIDTH - 1, :] = ub


def _conv_sample(state, ca, cb, conv_w, conv_b, ln_g, ln_b, bb):
    nbatch, npast, ch = state.shape
    full = lambda a: pl.BlockSpec(a.shape, lambda i: (0,) * a.ndim)
    return pl.pallas_call(
        functools.partial(_conv_sample_kernel, bb=bb),
        grid=(nbatch // bb,),
        in_specs=[pl.BlockSpec((bb, npast, ch), lambda i: (i, 0, 0)),
                  pl.BlockSpec((bb, ch), lambda i: (i, 0)),
                  pl.BlockSpec((bb, ch), lambda i: (i, 0)),
                  full(conv_w), full(conv_b), full(ln_g), full(ln_b)],
        out_specs=[pl.BlockSpec((bb, ch), lambda i: (i, 0)),
                   pl.BlockSpec((bb, npast, ch), lambda i: (i, 0, 0))],
        out_shape=[jax.ShapeDtypeStruct((nbatch, ch), BF16),
                   jax.ShapeDtypeStruct((nbatch, npast, ch), F32)],
        compiler_params=_cparams("parallel"),
        name="conv_sample",
    )(state, ca, cb, conv_w, conv_b, ln_g, ln_b)


def _merge_kernel(*refs, n_gate_blocks):
    x_ref, o_ref, c_ref = refs[0:3]
    ga_refs = refs[3:3 + n_gate_blocks]
    gc_refs = refs[3 + n_gate_blocks:3 + 2 * n_gate_blocks]
    wap_ref, wcp_ref, wout_ref, g_ref, b_ref, out_ref = refs[3 + 2 * n_gate_blocks:]
    a = jnp.dot(o_ref[...], wap_ref[...], preferred_element_type=F32)
    cc = jnp.dot(c_ref[...], wcp_ref[...], preferred_element_type=F32)
    ga = jnp.concatenate([r[...] for r in ga_refs], axis=-1)
    gc = jnp.concatenate([r[...] for r in gc_refs], axis=-1)
    mixed = _sigmoid(ga) * a + _sigmoid(gc) * cc
    y = jnp.dot(mixed.astype(BF16), wout_ref[...], preferred_element_type=F32)
    out_ref[...] = _layer_norm(DEEPNORM_ALPHA * x_ref[...] + y, g_ref[...], b_ref[...])


def _merge(x2d, o2d, c2d, z2d, wap, wcp, wout, ln_g, ln_b, tm):
    m, d = x2d.shape
    gw = 512
    n_gate_blocks = d // gw
    ga_col = (z2d.shape[1] - 2 * d) // gw
    gc_col = ga_col + n_gate_blocks
    row = lambda w: pl.BlockSpec((tm, w), lambda i: (i, 0))
    zblk = lambda col: pl.BlockSpec((tm, gw), lambda i: (i, col))
    full = lambda a: pl.BlockSpec(a.shape, lambda i: (0,) * a.ndim)
    in_specs = ([row(d), row(o2d.shape[1]), row(c2d.shape[1])]
                + [zblk(ga_col + k) for k in range(n_gate_blocks)]
                + [zblk(gc_col + k) for k in range(n_gate_blocks)]
                + [full(wap), full(wcp), full(wout), full(ln_g), full(ln_b)])
    args = [x2d, o2d, c2d] + [z2d] * (2 * n_gate_blocks) + [wap, wcp, wout, ln_g, ln_b]
    return pl.pallas_call(
        functools.partial(_merge_kernel, n_gate_blocks=n_gate_blocks),
        grid=(m // tm,),
        in_specs=in_specs,
        out_specs=pl.BlockSpec((tm, d), lambda i: (i, 0)),
        out_shape=jax.ShapeDtypeStruct((m, d), F32),
        compiler_params=_cparams("parallel"),
        name="merge",
    )(*args)


def _router_kernel(xp_ref, xs_ref, wh_ref, wl_ref, b_ref, x_ref, idx_ref, gw_ref, rank_ref, cnt_ref, carry,
                   *, n_prompt_tiles):
    @pl.when(pl.program_id(0) == 0)
    def _():
        carry[...] = jnp.zeros(carry.shape, F32)

    x = jnp.where(pl.program_id(0) < n_prompt_tiles, xp_ref[...], xs_ref[...])
    x_ref[...] = x
    tr = x.shape[0]
    xh = x.astype(BF16)
    xl = (x - xh.astype(F32)).astype(BF16)
    wh = wh_ref[...]
    logits = (jnp.dot(xh, wh, preferred_element_type=F32)
              + jnp.dot(xl, wh, preferred_element_type=F32)
              + jnp.dot(xh, wl_ref[...], preferred_element_type=F32))
    scores = _sigmoid(logits)
    sel = scores + b_ref[...]
    lane = lax.broadcasted_iota(jnp.int32, (tr, N_EXPERTS), 1)
    grp = lane // GROUP_SIZE

    gscore = []
    for g in range(N_EXPERT_GROUPS):
        v = jnp.where(grp == g, sel, NEG_INF)
        m1 = jnp.max(v, axis=-1, keepdims=True)
        i1 = jnp.min(jnp.where(v == m1, lane, N_EXPERTS), axis=-1, keepdims=True)
        m2 = jnp.max(jnp.where(lane == i1, NEG_INF, v), axis=-1, keepdims=True)
        gscore.append(m1 + m2)

    keep = jnp.zeros((tr, N_EXPERTS), jnp.bool_)
    for g in range(N_EXPERT_GROUPS):
        beaten = jnp.zeros((tr, 1), jnp.int32)
        for o in range(N_EXPERT_GROUPS):
            if o == g:
                continue
            wins = (gscore[o] > gscore[g]) | ((gscore[o] == gscore[g]) & (o < g))
            beaten = beaten + wins.astype(jnp.int32)
        keep = keep | ((beaten < TOPK_GROUPS) & (grp == g))

    cur = jnp.where(keep, sel, NEG_INF)
    picked = []
    weights = []
    onehot = jnp.zeros((tr, N_EXPERTS), F32)
    for _ in range(TOP_K):
        m = jnp.max(cur, axis=-1, keepdims=True)
        ik = jnp.min(jnp.where(cur == m, lane, N_EXPERTS), axis=-1, keepdims=True)
        hit = lane == ik
        weights.append(jnp.sum(jnp.where(hit, scores, 0.0), axis=-1, keepdims=True))
        cur = jnp.where(hit, NEG_INF, cur)
        onehot = onehot + hit.astype(F32)
        picked.append(ik)
    wsum = weights[0]
    for w in weights[1:]:
        wsum = wsum + w

    ri = lax.broadcasted_iota(jnp.int32, (tr, tr), 0)
    ci = lax.broadcasted_iota(jnp.int32, (tr, tr), 1)
    tri = (ri > ci).astype(BF16)
    before = jnp.dot(tri, onehot.astype(BF16), preferred_element_type=F32) + carry[...]
    carry[...] = carry[...] + jnp.sum(onehot, axis=0, keepdims=True)
    cnt_ref[...] = carry[...]

    slot = lax.broadcasted_iota(jnp.int32, (tr, LANES), 1)
    idx_out = jnp.zeros((tr, LANES), jnp.int32)
    rank_out = jnp.zeros((tr, LANES), jnp.int32)
    gw_out = jnp.zeros((tr, LANES), F32)
    for k in range(TOP_K):
        rk = jnp.sum(jnp.where(lane == picked[k], before, 0.0), axis=-1, keepdims=True)
        idx_out = jnp.where(slot == k, picked[k], idx_out)
        rank_out = jnp.where(slot == k, rk.astype(jnp.int32), rank_out)
        gw_out = jnp.where(slot == k, weights[k] / wsum * ROUTED_SCALE, gw_out)
    idx_ref[...] = idx_out
    rank_ref[...] = rank_out
    gw_ref[...] = gw_out


def _router(x1_prompt, x1_sample, wr_hi, wr_lo, b_router):
    d = x1_prompt.shape[1]
    tr = TOKEN_TILE
    npt = x1_prompt.shape[0] // tr
    t = x1_prompt.shape[0] + x1_sample.shape[0]
    full = lambda a: pl.BlockSpec(a.shape, lambda i: (0,) * a.ndim)
    tile = pl.BlockSpec((tr, LANES), lambda i: (i, 0))
    return pl.pallas_call(
        functools.partial(_router_kernel, n_prompt_tiles=npt),
        grid=(t // tr,),
        in_specs=[pl.BlockSpec((tr, d), lambda i: (jnp.minimum(i, npt - 1), 0)),
                  pl.BlockSpec((tr, d), lambda i: (jnp.maximum(i - npt, 0), 0)),
                  full(wr_hi), full(wr_lo), full(b_router)],
        out_specs=[pl.BlockSpec((tr, d), lambda i: (i, 0)), tile, tile, tile,
                   pl.BlockSpec((1, N_EXPERTS), lambda i: (0, 0))],
        out_shape=[jax.ShapeDtypeStruct((t, d), F32),
                   jax.ShapeDtypeStruct((t, LANES), jnp.int32),
                   jax.ShapeDtypeStruct((t, LANES), F32),
                   jax.ShapeDtypeStruct((t, LANES), jnp.int32),
                   jax.ShapeDtypeStruct((1, N_EXPERTS), F32)],
        scratch_shapes=[pltpu.VMEM((1, N_EXPERTS), F32)],
        compiler_params=_cparams("arbitrary"),
        name="router",
    )(x1_prompt, x1_sample, wr_hi, wr_lo, b_router)


PAD_CHUNKS = (64, 32, 16, 8)


def _dispatch_kernel(pstart_ref, cnt_ref, nact_ref, dest_ref, x_hbm, xs_hbm, zbuf, sem, zsem, *, n_blocks):
    i = pl.program_id(0)
    tt = TOKEN_TILE

    def row_fill(row):
        return pltpu.make_async_copy(zbuf.at[pl.ds(0, 1), :], xs_hbm.at[pl.ds(row, 1), :], zsem)

    def chunk_fill(row, p):
        return pltpu.make_async_copy(zbuf.at[pl.ds(0, p), :],
                                     xs_hbm.at[pl.ds(pl.multiple_of(row, SUBLANES), p), :], zsem)

    def block_fill(b):
        return pltpu.make_async_copy(
            zbuf, xs_hbm.at[pl.ds(pl.multiple_of(b * MOE_BLOCK, MOE_BLOCK), MOE_BLOCK), :], zsem)

    def pad_fills(e, act):
        n = cnt_ref[e]
        base = pstart_ref[e] + n
        padn = (MOE_BLOCK - (n & (MOE_BLOCK - 1))) & (MOE_BLOCK - 1)
        head = (SUBLANES - (n & (SUBLANES - 1))) & (SUBLANES - 1)
        for q in range(SUBLANES - 1):
            @pl.when(q < head)
            def _():
                act(row_fill(base + q))
        row = base + head
        rem = padn - head
        for p in PAD_CHUNKS:
            @pl.when((rem & p) != 0)
            def _():
                act(chunk_fill(row, p))
            row = row + (rem & p)

    @pl.when(i == 0)
    def _():
        zbuf[...] = jnp.zeros(zbuf.shape, zbuf.dtype)

        def start_e(e, c):
            pad_fills(e, lambda cp: cp.start())
            return c

        def wait_e(e, c):
            pad_fills(e, lambda cp: cp.wait())
            return c

        def start_b(b, c):
            block_fill(b).start()
            return c

        def wait_b(b, c):
            block_fill(b).wait()
            return c

        lax.fori_loop(0, N_EXPERTS, start_e, 0)
        lax.fori_loop(nact_ref[0], n_blocks, start_b, 0)
        lax.fori_loop(0, N_EXPERTS, wait_e, 0)
        lax.fori_loop(nact_ref[0], n_blocks, wait_b, 0)

    def token(t, c):
        src = x_hbm.at[pl.ds(i * tt + t, 1), :]
        for j in range(TOP_K):
            d = dest_ref[0, 0, t * TOP_K + j]
            pltpu.make_async_copy(src, xs_hbm.at[pl.ds(d, 1), :], sem).start()
        return c

    lax.fori_loop(0, tt, token, 0)
    pltpu.make_async_copy(x_hbm.at[pl.ds(0, tt * TOP_K), :], xs_hbm.at[pl.ds(0, tt * TOP_K), :], sem).wait()


def _dispatch(x1, dest3, pstart, cnt, nact, n_blocks):
    t, d = x1.shape
    nt = t // TOKEN_TILE
    return pl.pallas_call(
        functools.partial(_dispatch_kernel, n_blocks=n_blocks),
        grid_spec=pltpu.PrefetchScalarGridSpec(
            num_scalar_prefetch=3,
            grid=(nt,),
            in_specs=[pl.BlockSpec((1, 1, TOKEN_TILE * TOP_K), lambda i, *_: (i, 0, 0),
                                   memory_space=pltpu.SMEM),
                      pl.BlockSpec(memory_space=pl.ANY)],
            out_specs=pl.BlockSpec(memory_space=pl.ANY),
            scratch_shapes=[pltpu.VMEM((MOE_BLOCK, d), F32),
                            pltpu.SemaphoreType.DMA, pltpu.SemaphoreType.DMA]),
        out_shape=jax.ShapeDtypeStruct((n_blocks * MOE_BLOCK, d), F32),
        compiler_params=pltpu.CompilerParams(dimension_semantics=("arbitrary",),
                                             vmem_limit_bytes=VMEM_LIMIT, has_side_effects=True),
        name="dispatch",
    )(pstart, cnt, nact, dest3, x1)


def _experts_kernel(bexp_ref, nact_ref, xs_ref, wg_ref, wu_ref, wd_ref, y_ref, wg_bf, wu_bf, wd_bf):
    b = pl.program_id(0)
    active = b < nact_ref[0]
    new_expert = (b == 0) | (bexp_ref[b] != bexp_ref[jnp.maximum(b - 1, 0)])

    @pl.when(active & new_expert)
    def _():
        wg_bf[...] = wg_ref[...].astype(BF16)
        wu_bf[...] = wu_ref[...].astype(BF16)
        wd_bf[...] = wd_ref[...].astype(BF16)

    @pl.when(active)
    def _():
        x = xs_ref[...].astype(BF16)
        g = jnp.dot(x, wg_bf[...], preferred_element_type=F32)
        u = jnp.dot(x, wu_bf[...], preferred_element_type=F32)
        h = (_silu(g) * u).astype(BF16)
        y_ref[...] = jnp.dot(h, wd_bf[...], preferred_element_type=F32)

    @pl.when(jnp.logical_not(active))
    def _():
        y_ref[...] = jnp.zeros(y_ref.shape, y_ref.dtype)


def _experts(xs, w_gate_e, w_up_e, w_down_e, block_exp, nact):
    p, d = xs.shape
    ff = w_gate_e.shape[2]
    n_blocks = p // MOE_BLOCK
    live = lambda b, bexp, nact: (jnp.minimum(b, nact[0] - 1), 0)
    wmap = lambda b, bexp, nact: (bexp[b], 0, 0)
    return pl.pallas_call(
        _experts_kernel,
        grid_spec=pltpu.PrefetchScalarGridSpec(
            num_scalar_prefetch=2,
            grid=(n_blocks,),
            in_specs=[pl.BlockSpec((MOE_BLOCK, d), live),
                      pl.BlockSpec((None, d, ff), wmap),
                      pl.BlockSpec((None, d, ff), wmap),
                      pl.BlockSpec((None, ff, d), wmap)],
            out_specs=pl.BlockSpec((MOE_BLOCK, d), lambda b, bexp, nact: (b, 0)),
            scratch_shapes=[pltpu.VMEM((d, ff), BF16), pltpu.VMEM((d, ff), BF16), pltpu.VMEM((ff, d), BF16)]),
        out_shape=jax.ShapeDtypeStruct((p, d), F32),
        compiler_params=_cparams("arbitrary"),
        name="experts",
    )(block_exp, nact, xs, w_gate_e, w_up_e, w_down_e)


def _combine_kernel(dest_ref, x_ref, gw_ref, y_hbm, wgs_ref, wus_ref, wds_ref, g_ref, b_ref,
                    op_ref, os_ref, ybuf, sem, *, n_prompt_tiles):
    i = pl.program_id(0)
    tt = TOKEN_TILE

    def token(t, c):
        for j in range(TOP_K):
            d = dest_ref[0, 0, t * TOP_K + j]
            pltpu.make_async_copy(y_hbm.at[pl.ds(d, 1), :], ybuf.at[j, pl.ds(t, 1), :], sem).start()
        return c

    lax.fori_loop(0, tt, token, 0)

    x = x_ref[...]
    xb = x.astype(BF16)
    hs = _silu(jnp.dot(xb, wgs_ref[...], preferred_element_type=F32)) * jnp.dot(
        xb, wus_ref[...], preferred_element_type=F32)
    f = jnp.dot(hs.astype(BF16), wds_ref[...], preferred_element_type=F32)

    for j in range(TOP_K):
        pltpu.make_async_copy(y_hbm.at[pl.ds(0, tt), :], ybuf.at[j], sem).wait()
    gw = gw_ref[...]
    for j in range(TOP_K):
        f = f + gw[:, j:j + 1] * ybuf[j]
    out = _layer_norm(DEEPNORM_ALPHA * x + f, g_ref[...], b_ref[...])

    @pl.when(i < n_prompt_tiles)
    def _():
        op_ref[...] = out

    @pl.when(i >= n_prompt_tiles)
    def _():
        os_ref[...] = out


def _combine(x1, dest3, gw, y, wgs, wus, wds, ln_g, ln_b, n_prompt_rows):
    t, d = x1.shape
    tt = TOKEN_TILE
    nt = t // tt
    npt = n_prompt_rows // tt
    full = lambda a: pl.BlockSpec(a.shape, lambda i, *_: (0,) * a.ndim)
    return pl.pallas_call(
        functools.partial(_combine_kernel, n_prompt_tiles=npt),
        grid=(nt,),
        in_specs=[pl.BlockSpec((1, 1, tt * TOP_K), lambda i: (i, 0, 0), memory_space=pltpu.SMEM),
                  pl.BlockSpec((tt, d), lambda i: (i, 0)),
                  pl.BlockSpec((tt, LANES), lambda i: (i, 0)),
                  pl.BlockSpec(memory_space=pl.ANY),
                  full(wgs), full(wus), full(wds), full(ln_g), full(ln_b)],
        out_specs=[pl.BlockSpec((tt, d), lambda i: (jnp.minimum(i, npt - 1), 0)),
                   pl.BlockSpec((tt, d), lambda i: (jnp.maximum(i - npt, 0), 0))],
        out_shape=[jax.ShapeDtypeStruct((n_prompt_rows, d), F32),
                   jax.ShapeDtypeStruct((t - n_prompt_rows, d), F32)],
        scratch_shapes=[pltpu.VMEM((TOP_K, tt, d), F32), pltpu.SemaphoreType.DMA],
        compiler_params=_cparams("arbitrary"),
        name="combine",
    )(dest3, x1, gw, y, wgs, wus, wds, ln_g, ln_b)


def _moe_plan(idx, rank, counts, n_tokens):
    cnt = counts[0].astype(jnp.int32)
    padded = ((cnt + MOE_BLOCK - 1) // MOE_BLOCK) * MOE_BLOCK
    pend = jnp.cumsum(padded)
    pstart = pend - padded
    n_assign = n_tokens * TOP_K
    n_blocks = (n_assign + N_EXPERTS * (MOE_BLOCK - 1)) // MOE_BLOCK
    dest = pstart[idx[:, :TOP_K]] + rank[:, :TOP_K]
    dest3 = dest.reshape(n_tokens // TOKEN_TILE, 1, TOKEN_TILE * TOP_K)
    nact = (pend[-1:] // MOE_BLOCK).astype(jnp.int32)
    block_start = jnp.arange(n_blocks, dtype=jnp.int32) * MOE_BLOCK
    block_exp = jnp.minimum(jnp.sum(pend[None, :] <= block_start[:, None], axis=1), N_EXPERTS - 1)
    return dest3, pstart.astype(jnp.int32), cnt, nact, block_exp.astype(jnp.int32), n_blocks


def kernel(x_prompt, x_sample, cache_k, cache_v, state_conv, w_in, sinks, w_attn_proj, conv_w, conv_b,
           conv_ln_g, conv_ln_b, w_conv_proj, w_out, ln1_g, ln1_b, w_router, b_router, w_gate_e, w_up_e,
           w_down_e, w_gate_s, w_up_s, w_down_s, ln2_g, ln2_b):
    assert w_in.shape[0] == DEPTH
    batch, seq, d = x_prompt.shape
    nbatch = x_sample.shape[0]
    n_prompt = batch * seq
    n_tokens = n_prompt + nbatch
    ch = conv_w.shape[2]
    row = lambda a: a[0].reshape(1, -1)

    w_in_bf = w_in[0].astype(BF16)
    wap = w_attn_proj[0].astype(BF16)
    wcp = w_conv_proj[0].astype(BF16)
    wout = w_out[0].astype(BF16)
    wgs = w_gate_s[0].astype(BF16)
    wus = w_up_s[0].astype(BF16)
    wds = w_down_s[0].astype(BF16)
    wr = w_router[0].astype(F32)
    wr_hi = wr.astype(BF16)
    wr_lo = (wr - wr_hi.astype(F32)).astype(BF16)

    xp2d = x_prompt.reshape(n_prompt, d)
    xs2d = x_sample.reshape(nbatch, d)

    zp = _inproj(xp2d, w_in_bf, tm=1024, tn=512)
    o_p = _attn_prompt(zp, sinks[0], batch, seq)
    c_p, newconv_p = _conv_prompt(zp, conv_w[0], row(conv_b), row(conv_ln_g), row(conv_ln_b), batch, seq, ts=256)
    x1_p = _merge(xp2d, o_p, c_p, zp, wap, wcp, wout, row(ln1_g), row(ln1_b), tm=256)
    k_off, v_off, ca_off = ATTN_WIDTH, ATTN_WIDTH + KV_WIDTH, ATTN_WIDTH + 2 * KV_WIDTH
    zp3 = zp.reshape(batch, seq, -1)
    newk_p = zp3[:, seq - WINDOW:, k_off:v_off].reshape(1, batch, WINDOW, N_KV_HEADS, HEAD_DIM)
    newv_p = zp3[:, seq - WINDOW:, v_off:ca_off].reshape(1, batch, WINDOW, N_KV_HEADS, HEAD_DIM)

    zs = _inproj(xs2d, w_in_bf, tm=nbatch, tn=512)
    q3 = zs[:, :k_off].reshape(nbatch, N_HEADS, HEAD_DIM)
    kn3 = zs[:, k_off:v_off].reshape(nbatch, 1, KV_WIDTH)
    vn3 = zs[:, v_off:ca_off].reshape(nbatch, 1, KV_WIDTH)
    slopes_col = jnp.asarray(ALIBI_SLOPES, F32).reshape(N_HEADS, 1)
    o_s3, newk_s, newv_s = _attn_sample(
        q3, kn3, vn3, cache_k[0].reshape(nbatch, WINDOW, KV_WIDTH), cache_v[0].reshape(nbatch, WINDOW, KV_WIDTH),
        sinks[0].reshape(N_HEADS, 1), slopes_col, bb=8)
    c_s, newconv_s = _conv_sample(state_conv[0], zs[:, ca_off:ca_off + ch], zs[:, ca_off + ch:ca_off + 2 * ch],
                                  conv_w[0], row(conv_b), row(conv_ln_g), row(conv_ln_b), bb=8)
    x1_s = _merge(xs2d, o_s3.reshape(nbatch, ATTN_WIDTH), c_s, zs, wap, wcp, wout, row(ln1_g), row(ln1_b),
                  tm=nbatch)

    x1, idx, gw, rank, counts = _router(x1_p, x1_s, wr_hi, wr_lo, row(b_router))
    dest3, pstart, cnt, nact, block_exp, n_blocks = _moe_plan(idx, rank, counts, n_tokens)
    xs_sorted = _dispatch(x1, dest3, pstart, cnt, nact, n_blocks)
    y_sorted = _experts(xs_sorted, w_gate_e[0], w_up_e[0], w_down_e[0], block_exp, nact)
    y_p, y_s = _combine(x1, dest3, gw, y_sorted, wgs, wus, wds, row(ln2_g), row(ln2_b), n_prompt)

    return (y_p.reshape(batch, seq, d), y_s.reshape(nbatch, 1, d),
            newk_p, newv_p, newconv_p.reshape(1, batch, CONV_WIDTH - 1, ch),
            newk_s.reshape(1, nbatch, WINDOW, N_KV_HEADS, HEAD_DIM),
            newv_s.reshape(1, nbatch, WINDOW, N_KV_HEADS, HEAD_DIM),
            newconv_s.reshape(1, nbatch, CONV_WIDTH - 1, ch))
```

```python
import functools

import jax
import jax.numpy as jnp
from jax import lax
from jax.experimental import pallas as pl
from jax.experimental.pallas import tpu as pltpu

HEAD_DIM = 64
N_HEADS = 16
N_KV_HEADS = 4
GQA_GROUP = N_HEADS // N_KV_HEADS
WINDOW = 128
ATTN_WIDTH = N_HEADS * HEAD_DIM
KV_WIDTH = N_KV_HEADS * HEAD_DIM
CONV_WIDTH = 31
N_EXPERTS = 256
TOP_K = 8
N_EXPERT_GROUPS = 8
GROUP_SIZE = N_EXPERTS // N_EXPERT_GROUPS
TOPK_GROUPS = 4
ROUTED_SCALE = 2.5
LN_EPS = 1e-5
DEPTH = 1
DEEPNORM_ALPHA = (2 * DEPTH) ** 0.25
ALIBI_SLOPES = tuple(2.0 ** (-8.0 * (h + 1.0) / N_HEADS) for h in range(N_HEADS))
ATTN_SCALE = HEAD_DIM ** -0.5

MOE_BLOCK = 128
TOKEN_TILE = 128
LANES = 128
SUBLANES = 8
VMEM_LIMIT = 56 * 1024 * 1024

BF16 = jnp.bfloat16
F32 = jnp.float32
NEG_INF = float("-inf")


def _cparams(*sem):
    return pltpu.CompilerParams(dimension_semantics=sem, vmem_limit_bytes=VMEM_LIMIT)


def _sigmoid(x):
    return 1.0 / (1.0 + jnp.exp(-x))


def _silu(x):
    return x * _sigmoid(x)


def _layer_norm(x, g, b):
    mu = jnp.mean(x, axis=-1, keepdims=True)
    xc = x - mu
    var = jnp.mean(xc * xc, axis=-1, keepdims=True)
    return xc * lax.rsqrt(var + LN_EPS) * g + b


def _inproj_kernel(x_ref, w_ref, o_ref, xb_ref):
    @pl.when(pl.program_id(1) == 0)
    def _():
        xb_ref[...] = x_ref[...].astype(BF16)

    o_ref[...] = jnp.dot(xb_ref[...], w_ref[...], preferred_element_type=F32)


def _inproj(x2d, w_bf, tm, tn):
    m, k = x2d.shape
    n = w_bf.shape[1]
    return pl.pallas_call(
        _inproj_kernel,
        grid=(m // tm, n // tn),
        in_specs=[pl.BlockSpec((tm, k), lambda i, j: (i, 0)),
                  pl.BlockSpec((k, tn), lambda i, j: (0, j))],
        out_specs=pl.BlockSpec((tm, tn), lambda i, j: (i, j)),
        out_shape=jax.ShapeDtypeStruct((m, n), F32),
        scratch_shapes=[pltpu.VMEM((tm, k), BF16)],
        compiler_params=_cparams("parallel", "arbitrary"),
        name="inproj",
    )(x2d, w_bf)


def _attn_prompt_kernel(sinks_ref, q_ref, kp_ref, kc_ref, vp_ref, vc_ref, o_ref):
    i = pl.program_id(1)
    q = q_ref[...]
    k = jnp.concatenate([kp_ref[...], kc_ref[...]], axis=0).astype(BF16)
    v = jnp.concatenate([vp_ref[...], vc_ref[...]], axis=0).astype(BF16)
    qi = lax.broadcasted_iota(jnp.int32, (WINDOW, 2 * WINDOW), 0)
    kj = lax.broadcasted_iota(jnp.int32, (WINDOW, 2 * WINDOW), 1)
    dist = qi + WINDOW - kj
    valid = (dist >= 0) & (dist <= WINDOW) & ((kj >= WINDOW) | (i > 0))
    distf = dist.astype(F32)
    outs = []
    for h in range(N_HEADS):
        kv = h // GQA_GROUP
        qh = q[:, h * HEAD_DIM:(h + 1) * HEAD_DIM].astype(BF16)
        kh = k[:, kv * HEAD_DIM:(kv + 1) * HEAD_DIM]
        vh = v[:, kv * HEAD_DIM:(kv + 1) * HEAD_DIM]
        s = lax.dot_general(qh, kh, (((1,), (1,)), ((), ())), preferred_element_type=F32) * ATTN_SCALE
        s = jnp.where(valid, s - ALIBI_SLOPES[h] * distf, NEG_INF)
        sink = sinks_ref[h]
        m = jnp.maximum(jnp.max(s, axis=-1, keepdims=True), sink)
        e = jnp.exp(s - m)
        denom = jnp.sum(e, axis=-1, keepdims=True) + jnp.exp(sink - m)
        p = (e / denom).astype(BF16)
        outs.append(jnp.dot(p, vh, preferred_element_type=F32))
    o_ref[...] = jnp.concatenate(outs, axis=-1).astype(o_ref.dtype)


def _attn_prompt(z2d, sinks, batch, seq):
    nb = seq // WINDOW
    kcol = ATTN_WIDTH // KV_WIDTH
    vcol = kcol + 1

    def cur(col):
        return lambda b, i: (b * nb + i, col)

    def prev(col):
        return lambda b, i: (b * nb + jnp.maximum(i - 1, 0), col)

    return pl.pallas_call(
        _attn_prompt_kernel,
        grid=(batch, nb),
        in_specs=[pl.BlockSpec(memory_space=pltpu.SMEM),
                  pl.BlockSpec((WINDOW, ATTN_WIDTH), cur(0)),
                  pl.BlockSpec((WINDOW, KV_WIDTH), prev(kcol)),
                  pl.BlockSpec((WINDOW, KV_WIDTH), cur(kcol)),
                  pl.BlockSpec((WINDOW, KV_WIDTH), prev(vcol)),
                  pl.BlockSpec((WINDOW, KV_WIDTH), cur(vcol))],
        out_specs=pl.BlockSpec((WINDOW, ATTN_WIDTH), cur(0)),
        out_shape=jax.ShapeDtypeStruct((batch * seq, ATTN_WIDTH), BF16),
        compiler_params=_cparams("parallel", "parallel"),
        name="attn_prompt",
    )(sinks, z2d, z2d, z2d, z2d, z2d)


def _attn_sample_kernel(sinks_ref, slopes_ref, q_ref, kn_ref, vn_ref, ck_ref, cv_ref,
                        o_ref, nk_ref, nv_ref, *, bb):
    dist_c = (WINDOW - lax.broadcasted_iota(jnp.int32, (1, WINDOW), 1)).astype(F32)
    for b in range(bb):
        q = q_ref[b]
        kc = ck_ref[b]
        vc = cv_ref[b]
        kn = kn_ref[b]
        vn = vn_ref[b]
        for kv in range(N_KV_HEADS):
            rows = slice(kv * GQA_GROUP, (kv + 1) * GQA_GROUP)
            cols = slice(kv * HEAD_DIM, (kv + 1) * HEAD_DIM)
            qh = q[rows, :].astype(BF16)
            sink = sinks_ref[rows, :]
            slope = slopes_ref[rows, :]
            sc = lax.dot_general(qh, kc[:, cols].astype(BF16), (((1,), (1,)), ((), ())),
                                 preferred_element_type=F32) * ATTN_SCALE - slope * dist_c
            knh = kn[:, cols].astype(BF16).astype(F32)
            sn = jnp.sum(qh.astype(F32) * knh, axis=-1, keepdims=True) * ATTN_SCALE
            m = jnp.maximum(jnp.maximum(jnp.max(sc, axis=-1, keepdims=True), sn), sink)
            ec = jnp.exp(sc - m)
            en = jnp.exp(sn - m)
            denom = jnp.sum(ec, axis=-1, keepdims=True) + en + jnp.exp(sink - m)
            pc = (ec / denom).astype(BF16)
            pn = (en / denom).astype(BF16).astype(F32)
            vnh = vn[:, cols].astype(BF16).astype(F32)
            oh = jnp.dot(pc, vc[:, cols].astype(BF16), preferred_element_type=F32) + pn * vnh
            o_ref[b, rows, :] = oh.astype(o_ref.dtype)
        nk_ref[b, 0:WINDOW - 1, :] = kc[1:WINDOW, :]
        nk_ref[b, WINDOW - 1:WINDOW, :] = kn
        nv_ref[b, 0:WINDOW - 1, :] = vc[1:WINDOW, :]
        nv_ref[b, WINDOW - 1:WINDOW, :] = vn


def _attn_sample(q3, kn3, vn3, ck, cv, sinks_col, slopes_col, bb):
    nbatch = q3.shape[0]
    blk3 = lambda s1, s2: pl.BlockSpec((bb, s1, s2), lambda i: (i, 0, 0))
    full = lambda a: pl.BlockSpec(a.shape, lambda i: (0,) * a.ndim)
    return pl.pallas_call(
        functools.partial(_attn_sample_kernel, bb=bb),
        grid=(nbatch // bb,),
        in_specs=[full(sinks_col), full(slopes_col),
                  blk3(N_HEADS, HEAD_DIM), blk3(1, KV_WIDTH), blk3(1, KV_WIDTH),
                  blk3(WINDOW, KV_WIDTH), blk3(WINDOW, KV_WIDTH)],
        out_specs=[blk3(N_HEADS, HEAD_DIM), blk3(WINDOW, KV_WIDTH), blk3(WINDOW, KV_WIDTH)],
        out_shape=[jax.ShapeDtypeStruct((nbatch, N_HEADS, HEAD_DIM), BF16),
                   jax.ShapeDtypeStruct((nbatch, WINDOW, KV_WIDTH), F32),
                   jax.ShapeDtypeStruct((nbatch, WINDOW, KV_WIDTH), F32)],
        compiler_params=_cparams("parallel"),
        name="attn_sample",
    )(sinks_col, slopes_col, q3, kn3, vn3, ck, cv)


CONV_HALO = 32
CONV_ROWS = 64


def _conv_prompt_kernel(ca0, ca1, cb0, cb1, w_ref, b_ref, g_ref, beta_ref,
                        c_ref, nc_ref, ext, ybuf, *, ts):
    i = pl.program_id(1)
    half = ca0.shape[1]

    @pl.when(i == 0)
    def _():
        ext[0:CONV_HALO, :] = jnp.zeros((CONV_HALO, ext.shape[1]), F32)

    @pl.when(i > 0)
    def _():
        ext[0:CONV_HALO, :] = ext[ts:ts + CONV_HALO, :]

    ext[CONV_HALO:CONV_HALO + ts, 0:half] = ca0[...] * _sigmoid(cb0[...])
    ext[CONV_HALO:CONV_HALO + ts, half:2 * half] = ca1[...] * _sigmoid(cb1[...])

    shift = CONV_HALO - (CONV_WIDTH - 1)
    for r in range(ts // CONV_ROWS):
        for c in range(ext.shape[1] // LANES):
            cols = slice(c * LANES, (c + 1) * LANES)
            acc = jnp.zeros((CONV_ROWS, LANES), F32)
            for j in range(CONV_WIDTH):
                r0 = r * CONV_ROWS + shift + j
                acc = acc + w_ref[j:j + 1, cols] * ext[r0:r0 + CONV_ROWS, cols]
            ybuf[r * CONV_ROWS:(r + 1) * CONV_ROWS, cols] = acc

    y = ybuf[...] + b_ref[...]
    c_ref[...] = _silu(_layer_norm(y, g_ref[...], beta_ref[...])).astype(c_ref.dtype)

    @pl.when(i == pl.num_programs(1) - 1)
    def _():
        nc_ref[...] = ext[CONV_HALO + ts - (CONV_WIDTH - 1):CONV_HALO + ts, :]


def _conv_prompt(z2d, conv_w, conv_b, ln_g, ln_b, batch, seq, ts):
    nt = seq // ts
    ch = conv_w.shape[1]
    half = ch // 2
    ca_col = (ATTN_WIDTH + 2 * KV_WIDTH) // half
    cb_col = ca_col + 2
    zblk = lambda col: pl.BlockSpec((ts, half), lambda b, i: (b * nt + i, col))
    full = lambda a: pl.BlockSpec(a.shape, lambda b, i: (0,) * a.ndim)
    return pl.pallas_call(
        functools.partial(_conv_prompt_kernel, ts=ts),
        grid=(batch, nt),
        in_specs=[zblk(ca_col), zblk(ca_col + 1), zblk(cb_col), zblk(cb_col + 1),
                  full(conv_w), full(conv_b), full(ln_g), full(ln_b)],
        out_specs=[pl.BlockSpec((ts, ch), lambda b, i: (b * nt + i, 0)),
                   pl.BlockSpec((None, CONV_WIDTH - 1, ch), lambda b, i: (b, 0, 0))],
        out_shape=[jax.ShapeDtypeStruct((batch * seq, ch), BF16),
                   jax.ShapeDtypeStruct((batch, CONV_WIDTH - 1, ch), F32)],
        scratch_shapes=[pltpu.VMEM((CONV_HALO + ts, ch), F32), pltpu.VMEM((ts, ch), F32)],
        compiler_params=_cparams("parallel", "arbitrary"),
        name="conv_prompt",
    )(z2d, z2d, z2d, z2d, conv_w, conv_b, ln_g, ln_b)


def _conv_sample_kernel(st_ref, ca_ref, cb_ref, w_ref, b_ref, g_ref, beta_ref, c_ref, ns_ref, *, bb):
    u = ca_ref[...] * _sigmoid(cb_ref[...])
    w_past = w_ref[0:CONV_WIDTH - 1, :]
    w_new = w_ref[CONV_WIDTH - 1:CONV_WIDTH, :]
    for b in range(bb):
        st = st_ref[b]
        ub = u[b:b + 1, :]
        y = jnp.sum(st * w_past, axis=0, keepdims=True) + ub * w_new + b_ref[...]
        c_ref[b:b + 1, :] = _silu(_layer_norm(y, g_ref[...], beta_ref[...])).astype(c_ref.dtype)
        ns_ref[b, 0:CONV_WIDTH - 2, :] = st[1:CONV_WIDTH - 1, :]
        ns_ref[b, CONV_WIDTH - 2:CONV_WIDTH - 1, :] = ub


def _conv_sample(state, ca, cb, conv_w, conv_b, ln_g, ln_b, bb):
    nbatch, npast, ch = state.shape
    full = lambda a: pl.BlockSpec(a.shape, lambda i: (0,) * a.ndim)
    return pl.pallas_call(
        functools.partial(_conv_sample_kernel, bb=bb),
        grid=(nbatch // bb,),
        in_specs=[pl.BlockSpec((bb, npast, ch), lambda i: (i, 0, 0)),
                  pl.BlockSpec((bb, ch), lambda i: (i, 0)),
                  pl.BlockSpec((bb, ch), lambda i: (i, 0)),
                  full(conv_w), full(conv_b), full(ln_g), full(ln_b)],
        out_specs=[pl.BlockSpec((bb, ch), lambda i: (i, 0)),
                   pl.BlockSpec((bb, npast, ch), lambda i: (i, 0, 0))],
        out_shape=[jax.ShapeDtypeStruct((nbatch, ch), BF16),
                   jax.ShapeDtypeStruct((nbatch, npast, ch), F32)],
        compiler_params=_cparams("parallel"),
        name="conv_sample",
    )(state, ca, cb, conv_w, conv_b, ln_g, ln_b)


def _merge_kernel(*refs, n_gate_blocks):
    x_ref, o_ref, c_ref = refs[0:3]
    ga_refs = refs[3:3 + n_gate_blocks]
    gc_refs = refs[3 + n_gate_blocks:3 + 2 * n_gate_blocks]
    wap_ref, wcp_ref, wout_ref, g_ref, b_ref, out_ref = refs[3 + 2 * n_gate_blocks:]
    a = jnp.dot(o_ref[...], wap_ref[...], preferred_element_type=F32)
    cc = jnp.dot(c_ref[...], wcp_ref[...], preferred_element_type=F32)
    ga = jnp.concatenate([r[...] for r in ga_refs], axis=-1)
    gc = jnp.concatenate([r[...] for r in gc_refs], axis=-1)
    mixed = _sigmoid(ga) * a + _sigmoid(gc) * cc
    y = jnp.dot(mixed.astype(BF16), wout_ref[...], preferred_element_type=F32)
    out_ref[...] = _layer_norm(DEEPNORM_ALPHA * x_ref[...] + y, g_ref[...], b_ref[...])


def _merge(x2d, o2d, c2d, z2d, wap, wcp, wout, ln_g, ln_b, tm):
    m, d = x2d.shape
    gw = 512
    n_gate_blocks = d // gw
    ga_col = (z2d.shape[1] - 2 * d) // gw
    gc_col = ga_col + n_gate_blocks
    row = lambda w: pl.BlockSpec((tm, w), lambda i: (i, 0))
    zblk = lambda col: pl.BlockSpec((tm, gw), lambda i: (i, col))
    full = lambda a: pl.BlockSpec(a.shape, lambda i: (0,) * a.ndim)
    in_specs = ([row(d), row(o2d.shape[1]), row(c2d.shape[1])]
                + [zblk(ga_col + k) for k in range(n_gate_blocks)]
                + [zblk(gc_col + k) for k in range(n_gate_blocks)]
                + [full(wap), full(wcp), full(wout), full(ln_g), full(ln_b)])
    args = [x2d, o2d, c2d] + [z2d] * (2 * n_gate_blocks) + [wap, wcp, wout, ln_g, ln_b]
    return pl.pallas_call(
        functools.partial(_merge_kernel, n_gate_blocks=n_gate_blocks),
        grid=(m // tm,),
        in_specs=in_specs,
        out_specs=pl.BlockSpec((tm, d), lambda i: (i, 0)),
        out_shape=jax.ShapeDtypeStruct((m, d), F32),
        compiler_params=_cparams("parallel"),
        name="merge",
    )(*args)


def _router_kernel(xp_ref, xs_ref, wh_ref, wl_ref, b_ref, x_ref, idx_ref, gw_ref, rank_ref, cnt_ref, carry,
                   *, n_prompt_tiles):
    @pl.when(pl.program_id(0) == 0)
    def _():
        carry[...] = jnp.zeros(carry.shape, F32)

    x = jnp.where(pl.program_id(0) < n_prompt_tiles, xp_ref[...], xs_ref[...])
    x_ref[...] = x
    tr = x.shape[0]
    xh = x.astype(BF16)
    xl = (x - xh.astype(F32)).astype(BF16)
    wh = wh_ref[...]
    logits = (jnp.dot(xh, wh, preferred_element_type=F32)
              + jnp.dot(xl, wh, preferred_element_type=F32)
              + jnp.dot(xh, wl_ref[...], preferred_element_type=F32))
    scores = _sigmoid(logits)
    sel = scores + b_ref[...]
    lane = lax.broadcasted_iota(jnp.int32, (tr, N_EXPERTS), 1)
    grp = lane // GROUP_SIZE

    gscore = []
    for g in range(N_EXPERT_GROUPS):
        v = jnp.where(grp == g, sel, NEG_INF)
        m1 = jnp.max(v, axis=-1, keepdims=True)
        i1 = jnp.min(jnp.where(v == m1, lane, N_EXPERTS), axis=-1, keepdims=True)
        m2 = jnp.max(jnp.where(lane == i1, NEG_INF, v), axis=-1, keepdims=True)
        gscore.append(m1 + m2)

    keep = jnp.zeros((tr, N_EXPERTS), jnp.bool_)
    for g in range(N_EXPERT_GROUPS):
        beaten = jnp.zeros((tr, 1), jnp.int32)
        for o in range(N_EXPERT_GROUPS):
            if o == g:
                continue
            wins = (gscore[o] > gscore[g]) | ((gscore[o] == gscore[g]) & (o < g))
            beaten = beaten + wins.astype(jnp.int32)
        keep = keep | ((beaten < TOPK_GROUPS) & (grp == g))

    cur = jnp.where(keep, sel, NEG_INF)
    picked = []
    weights = []
    onehot = jnp.zeros((tr, N_EXPERTS), F32)
    for _ in range(TOP_K):
        m = jnp.max(cur, axis=-1, keepdims=True)
        ik = jnp.min(jnp.where(cur == m, lane, N_EXPERTS), axis=-1, keepdims=True)
        hit = lane == ik
        weights.append(jnp.sum(jnp.where(hit, scores, 0.0), axis=-1, keepdims=True))
        cur = jnp.where(hit, NEG_INF, cur)
        onehot = onehot + hit.astype(F32)
        picked.append(ik)
    wsum = weights[0]
    for w in weights[1:]:
        wsum = wsum + w

    ri = lax.broadcasted_iota(jnp.int32, (tr, tr), 0)
    ci = lax.broadcasted_iota(jnp.int32, (tr, tr), 1)
    tri = (ri > ci).astype(BF16)
    before = jnp.dot(tri, onehot.astype(BF16), preferred_element_type=F32) + carry[...]
    carry[...] = carry[...] + jnp.sum(onehot, axis=0, keepdims=True)
    cnt_ref[...] = carry[...]

    slot = lax.broadcasted_iota(jnp.int32, (tr, LANES), 1)
    idx_out = jnp.zeros((tr, LANES), jnp.int32)
    rank_out = jnp.zeros((tr, LANES), jnp.int32)
    gw_out = jnp.zeros((tr, LANES), F32)
    for k in range(TOP_K):
        rk = jnp.sum(jnp.where(lane == picked[k], before, 0.0), axis=-1, keepdims=True)
        idx_out = jnp.where(slot == k, picked[k], idx_out)
        rank_out = jnp.where(slot == k, rk.astype(jnp.int32), rank_out)
        gw_out = jnp.where(slot == k, weights[k] / wsum * ROUTED_SCALE, gw_out)
    idx_ref[...] = idx_out
    rank_ref[...] = rank_out
    gw_ref[...] = gw_out


def _router(x1_prompt, x1_sample, wr_hi, wr_lo, b_router):
    d = x1_prompt.shape[1]
    tr = TOKEN_TILE
    npt = x1_prompt.shape[0] // tr
    t = x1_prompt.shape[0] + x1_sample.shape[0]
    full = lambda a: pl.BlockSpec(a.shape, lambda i: (0,) * a.ndim)
    tile = pl.BlockSpec((tr, LANES), lambda i: (i, 0))
    return pl.pallas_call(
        functools.partial(_router_kernel, n_prompt_tiles=npt),
        grid=(t // tr,),
        in_specs=[pl.BlockSpec((tr, d), lambda i: (jnp.minimum(i, npt - 1), 0)),
                  pl.BlockSpec((tr, d), lambda i: (jnp.maximum(i - npt, 0), 0)),
                  full(wr_hi), full(wr_lo), full(b_router)],
        out_specs=[pl.BlockSpec((tr, d), lambda i: (i, 0)), tile, tile, tile,
                   pl.BlockSpec((1, N_EXPERTS), lambda i: (0, 0))],
        out_shape=[jax.ShapeDtypeStruct((t, d), F32),
                   jax.ShapeDtypeStruct((t, LANES), jnp.int32),
                   jax.ShapeDtypeStruct((t, LANES), F32),
                   jax.ShapeDtypeStruct((t, LANES), jnp.int32),
                   jax.ShapeDtypeStruct((1, N_EXPERTS), F32)],
        scratch_shapes=[pltpu.VMEM((1, N_EXPERTS), F32)],
        compiler_params=_cparams("arbitrary"),
        name="router",
    )(x1_prompt, x1_sample, wr_hi, wr_lo, b_router)


PAD_CHUNKS = (64, 32, 16, 8)


def _dispatch_kernel(pstart_ref, cnt_ref, nact_ref, idx_ref, rank_ref, x_ref, xs_hbm, zbuf, sem, zsem,
                     *, n_blocks):
    i = pl.program_id(0)
    tt = TOKEN_TILE

    def row_fill(row):
        return pltpu.make_async_copy(zbuf.at[pl.ds(0, 1), :], xs_hbm.at[pl.ds(row, 1), :], zsem)

    def chunk_fill(row, p):
        return pltpu.make_async_copy(zbuf.at[pl.ds(0, p), :],
                                     xs_hbm.at[pl.ds(pl.multiple_of(row, SUBLANES), p), :], zsem)

    def block_fill(b):
        return pltpu.make_async_copy(
            zbuf, xs_hbm.at[pl.ds(pl.multiple_of(b * MOE_BLOCK, MOE_BLOCK), MOE_BLOCK), :], zsem)

    def pad_fills(e, act):
        n = cnt_ref[e]
        base = pstart_ref[e] + n
        padn = (MOE_BLOCK - (n & (MOE_BLOCK - 1))) & (MOE_BLOCK - 1)
        head = (SUBLANES - (n & (SUBLANES - 1))) & (SUBLANES - 1)
        for q in range(SUBLANES - 1):
            @pl.when(q < head)
            def _():
                act(row_fill(base + q))
        row = base + head
        rem = padn - head
        for p in PAD_CHUNKS:
            @pl.when((rem & p) != 0)
            def _():
                act(chunk_fill(row, p))
            row = row + (rem & p)

    @pl.when(i == 0)
    def _():
        zbuf[...] = jnp.zeros(zbuf.shape, zbuf.dtype)

        def start_e(e, c):
            pad_fills(e, lambda cp: cp.start())
            return c

        def wait_e(e, c):
            pad_fills(e, lambda cp: cp.wait())
            return c

        def start_b(b, c):
            block_fill(b).start()
            return c

        def wait_b(b, c):
            block_fill(b).wait()
            return c

        lax.fori_loop(0, N_EXPERTS, start_e, 0)
        lax.fori_loop(nact_ref[0], n_blocks, start_b, 0)
        lax.fori_loop(0, N_EXPERTS, wait_e, 0)
        lax.fori_loop(nact_ref[0], n_blocks, wait_b, 0)

    def token(t, c):
        src = x_ref.at[pl.ds(t, 1), :]
        for j in range(TOP_K):
            a = t * TOP_K + j
            d = pstart_ref[idx_ref[0, 0, a]] + rank_ref[0, 0, a]
            pltpu.make_async_copy(src, xs_hbm.at[pl.ds(d, 1), :], sem).start()
        return c

    lax.fori_loop(0, tt, token, 0)
    for j in range(TOP_K):
        pltpu.make_async_copy(x_ref, xs_hbm.at[pl.ds(0, tt), :], sem).wait()


def _dispatch(x1, idx3, rank3, pstart, cnt, nact, n_blocks):
    t, d = x1.shape
    tt = TOKEN_TILE
    smem_tile = pl.BlockSpec((1, 1, tt * TOP_K), lambda i, *_: (i, 0, 0), memory_space=pltpu.SMEM)
    return pl.pallas_call(
        functools.partial(_dispatch_kernel, n_blocks=n_blocks),
        grid_spec=pltpu.PrefetchScalarGridSpec(
            num_scalar_prefetch=3,
            grid=(t // tt,),
            in_specs=[smem_tile, smem_tile, pl.BlockSpec((tt, d), lambda i, *_: (i, 0))],
            out_specs=pl.BlockSpec(memory_space=pl.ANY),
            scratch_shapes=[pltpu.VMEM((MOE_BLOCK, d), F32),
                            pltpu.SemaphoreType.DMA, pltpu.SemaphoreType.DMA]),
        out_shape=jax.ShapeDtypeStruct((n_blocks * MOE_BLOCK, d), F32),
        compiler_params=pltpu.CompilerParams(dimension_semantics=("arbitrary",),
                                             vmem_limit_bytes=VMEM_LIMIT, has_side_effects=True),
        name="dispatch",
    )(pstart, cnt, nact, idx3, rank3, x1)


def _experts_kernel(bexp_ref, nact_ref, xs_ref, wg_hbm, wu_hbm, wd_hbm, y_ref,
                    wg_f32, wu_f32, wd_f32, wg_bf, wu_bf, wd_bf, slot_ref, sems):
    b = pl.program_id(0)
    nact = nact_ref[0]
    last = pl.num_programs(0) - 1
    active = b < nact
    e = bexp_ref[b]
    new_expert = (b == 0) | (e != bexp_ref[jnp.maximum(b - 1, 0)])

    def fetch(expert, slot):
        return (pltpu.make_async_copy(wg_hbm.at[expert], wg_f32.at[slot], sems.at[slot, 0]),
                pltpu.make_async_copy(wu_hbm.at[expert], wu_f32.at[slot], sems.at[slot, 1]),
                pltpu.make_async_copy(wd_hbm.at[expert], wd_f32.at[slot], sems.at[slot, 2]))

    @pl.when(active & new_expert)
    def _():
        @pl.when(b == 0)
        def _():
            slot_ref[0] = 0
            for cp in fetch(e, 0):
                cp.start()

        @pl.when(b > 0)
        def _():
            slot_ref[0] = 1 - slot_ref[0]

        slot = slot_ref[0]
        nxt = lax.while_loop(lambda j: (j < nact) & (bexp_ref[jnp.minimum(j, last)] == e),
                             lambda j: j + 1, b + 1)

        @pl.when(nxt < nact)
        def _():
            for cp in fetch(bexp_ref[jnp.minimum(nxt, last)], 1 - slot):
                cp.start()

        for cp in fetch(e, slot):
            cp.wait()
        wg_bf[...] = wg_f32[slot].astype(BF16)
        wu_bf[...] = wu_f32[slot].astype(BF16)
        wd_bf[...] = wd_f32[slot].astype(BF16)

    @pl.when(active)
    def _():
        x = xs_ref[...].astype(BF16)
        g = jnp.dot(x, wg_bf[...], preferred_element_type=F32)
        u = jnp.dot(x, wu_bf[...], preferred_element_type=F32)
        h = (_silu(g) * u).astype(BF16)
        y_ref[...] = jnp.dot(h, wd_bf[...], preferred_element_type=F32)

    @pl.when(jnp.logical_not(active))
    def _():
        y_ref[...] = jnp.zeros(y_ref.shape, y_ref.dtype)


def _experts(xs, w_gate_e, w_up_e, w_down_e, block_exp, nact):
    p, d = xs.shape
    ff = w_gate_e.shape[2]
    n_blocks = p // MOE_BLOCK
    live = lambda b, bexp, nact: (jnp.minimum(b, nact[0] - 1), 0)
    hbm = pl.BlockSpec(memory_space=pl.ANY)
    return pl.pallas_call(
        _experts_kernel,
        grid_spec=pltpu.PrefetchScalarGridSpec(
            num_scalar_prefetch=2,
            grid=(n_blocks,),
            in_specs=[pl.BlockSpec((MOE_BLOCK, d), live), hbm, hbm, hbm],
            out_specs=pl.BlockSpec((MOE_BLOCK, d), lambda b, bexp, nact: (b, 0)),
            scratch_shapes=[pltpu.VMEM((2, d, ff), F32), pltpu.VMEM((2, d, ff), F32), pltpu.VMEM((2, ff, d), F32),
                            pltpu.VMEM((d, ff), BF16), pltpu.VMEM((d, ff), BF16), pltpu.VMEM((ff, d), BF16),
                            pltpu.SMEM((1,), jnp.int32), pltpu.SemaphoreType.DMA((2, 3))]),
        out_shape=jax.ShapeDtypeStruct((p, d), F32),
        compiler_params=_cparams("arbitrary"),
        name="experts",
    )(block_exp, nact, xs, w_gate_e, w_up_e, w_down_e)


def _combine_kernel(pstart_ref, idx_ref, rank_ref, x_ref, gw_ref, y_hbm, wgs_ref, wus_ref, wds_ref, g_ref, b_ref,
                    op_ref, os_ref, ybuf, sem, *, n_prompt_tiles):
    i = pl.program_id(0)
    tt = TOKEN_TILE

    def token(t, c):
        for j in range(TOP_K):
            a = t * TOP_K + j
            d = pstart_ref[idx_ref[0, 0, a]] + rank_ref[0, 0, a]
            pltpu.make_async_copy(y_hbm.at[pl.ds(d, 1), :], ybuf.at[j, pl.ds(t, 1), :], sem).start()
        return c

    lax.fori_loop(0, tt, token, 0)

    x = x_ref[...]
    xb = x.astype(BF16)
    hs = _silu(jnp.dot(xb, wgs_ref[...], preferred_element_type=F32)) * jnp.dot(
        xb, wus_ref[...], preferred_element_type=F32)
    f = jnp.dot(hs.astype(BF16), wds_ref[...], preferred_element_type=F32)

    for j in range(TOP_K):
        pltpu.make_async_copy(y_hbm.at[pl.ds(0, tt), :], ybuf.at[j], sem).wait()
    gw = gw_ref[...]
    for j in range(TOP_K):
        f = f + gw[:, j:j + 1] * ybuf[j]
    out = _layer_norm(DEEPNORM_ALPHA * x + f, g_ref[...], b_ref[...])

    @pl.when(i < n_prompt_tiles)
    def _():
        op_ref[...] = out

    @pl.when(i >= n_prompt_tiles)
    def _():
        os_ref[...] = out


def _combine(x1, idx3, rank3, pstart, gw, y, wgs, wus, wds, ln_g, ln_b, n_prompt_rows):
    t, d = x1.shape
    tt = TOKEN_TILE
    nt = t // tt
    npt = n_prompt_rows // tt
    full = lambda a: pl.BlockSpec(a.shape, lambda i, *_: (0,) * a.ndim)
    smem_tile = pl.BlockSpec((1, 1, tt * TOP_K), lambda i, *_: (i, 0, 0), memory_space=pltpu.SMEM)
    return pl.pallas_call(
        functools.partial(_combine_kernel, n_prompt_tiles=npt),
        grid_spec=pltpu.PrefetchScalarGridSpec(
            num_scalar_prefetch=1,
            grid=(nt,),
            in_specs=[smem_tile, smem_tile,
                      pl.BlockSpec((tt, d), lambda i, *_: (i, 0)),
                      pl.BlockSpec((tt, LANES), lambda i, *_: (i, 0)),
                      pl.BlockSpec(memory_space=pl.ANY),
                      full(wgs), full(wus), full(wds), full(ln_g), full(ln_b)],
            out_specs=[pl.BlockSpec((tt, d), lambda i, *_: (jnp.minimum(i, npt - 1), 0)),
                       pl.BlockSpec((tt, d), lambda i, *_: (jnp.maximum(i - npt, 0), 0))],
            scratch_shapes=[pltpu.VMEM((TOP_K, tt, d), F32), pltpu.SemaphoreType.DMA]),
        out_shape=[jax.ShapeDtypeStruct((n_prompt_rows, d), F32),
                   jax.ShapeDtypeStruct((t - n_prompt_rows, d), F32)],
        compiler_params=_cparams("arbitrary"),
        name="combine",
    )(pstart, idx3, rank3, x1, gw, y, wgs, wus, wds, ln_g, ln_b)


def _moe_plan(idx, rank, counts, n_tokens):
    cnt = counts[0].astype(jnp.int32)
    padded = ((cnt + MOE_BLOCK - 1) // MOE_BLOCK) * MOE_BLOCK
    pend = jnp.cumsum(padded)
    pstart = pend - padded
    n_assign = n_tokens * TOP_K
    n_blocks = (n_assign + N_EXPERTS * (MOE_BLOCK - 1)) // MOE_BLOCK
    tile3 = lambda a: a[:, :TOP_K].reshape(n_tokens // TOKEN_TILE, 1, TOKEN_TILE * TOP_K)
    nact = (pend[-1:] // MOE_BLOCK).astype(jnp.int32)
    block_start = jnp.arange(n_blocks, dtype=jnp.int32) * MOE_BLOCK
    block_exp = jnp.minimum(jnp.sum(pend[None, :] <= block_start[:, None], axis=1), N_EXPERTS - 1)
    return tile3(idx), tile3(rank), pstart.astype(jnp.int32), cnt, nact, block_exp.astype(jnp.int32), n_blocks


def kernel(x_prompt, x_sample, cache_k, cache_v, state_conv, w_in, sinks, w_attn_proj, conv_w, conv_b,
           conv_ln_g, conv_ln_b, w_conv_proj, w_out, ln1_g, ln1_b, w_router, b_router, w_gate_e, w_up_e,
           w_down_e, w_gate_s, w_up_s, w_down_s, ln2_g, ln2_b):
    assert w_in.shape[0] == DEPTH
    batch, seq, d = x_prompt.shape
    nbatch = x_sample.shape[0]
    n_prompt = batch * seq
    n_tokens = n_prompt + nbatch
    ch = conv_w.shape[2]
    row = lambda a: a[0].reshape(1, -1)

    w_in_bf = w_in[0].astype(BF16)
    wap = w_attn_proj[0].astype(BF16)
    wcp = w_conv_proj[0].astype(BF16)
    wout = w_out[0].astype(BF16)
    wgs = w_gate_s[0].astype(BF16)
    wus = w_up_s[0].astype(BF16)
    wds = w_down_s[0].astype(BF16)
    wr = w_router[0].astype(F32)
    wr_hi = wr.astype(BF16)
    wr_lo = (wr - wr_hi.astype(F32)).astype(BF16)

    xp2d = x_prompt.reshape(n_prompt, d)
    xs2d = x_sample.reshape(nbatch, d)

    zp = _inproj(xp2d, w_in_bf, tm=1024, tn=512)
    o_p = _attn_prompt(zp, sinks[0], batch, seq)
    c_p, newconv_p = _conv_prompt(zp, conv_w[0], row(conv_b), row(conv_ln_g), row(conv_ln_b), batch, seq, ts=256)
    x1_p = _merge(xp2d, o_p, c_p, zp, wap, wcp, wout, row(ln1_g), row(ln1_b), tm=256)
    k_off, v_off, ca_off = ATTN_WIDTH, ATTN_WIDTH + KV_WIDTH, ATTN_WIDTH + 2 * KV_WIDTH
    zp3 = zp.reshape(batch, seq, -1)
    newk_p = zp3[:, seq - WINDOW:, k_off:v_off].reshape(1, batch, WINDOW, N_KV_HEADS, HEAD_DIM)
    newv_p = zp3[:, seq - WINDOW:, v_off:ca_off].reshape(1, batch, WINDOW, N_KV_HEADS, HEAD_DIM)

    zs = _inproj(xs2d, w_in_bf, tm=nbatch, tn=512)
    q3 = zs[:, :k_off].reshape(nbatch, N_HEADS, HEAD_DIM)
    kn3 = zs[:, k_off:v_off].reshape(nbatch, 1, KV_WIDTH)
    vn3 = zs[:, v_off:ca_off].reshape(nbatch, 1, KV_WIDTH)
    slopes_col = jnp.asarray(ALIBI_SLOPES, F32).reshape(N_HEADS, 1)
    o_s3, newk_s, newv_s = _attn_sample(
        q3, kn3, vn3, cache_k[0].reshape(nbatch, WINDOW, KV_WIDTH), cache_v[0].reshape(nbatch, WINDOW, KV_WIDTH),
        sinks[0].reshape(N_HEADS, 1), slopes_col, bb=8)
    c_s, newconv_s = _conv_sample(state_conv[0], zs[:, ca_off:ca_off + ch], zs[:, ca_off + ch:ca_off + 2 * ch],
                                  conv_w[0], row(conv_b), row(conv_ln_g), row(conv_ln_b), bb=8)
    x1_s = _merge(xs2d, o_s3.reshape(nbatch, ATTN_WIDTH), c_s, zs, wap, wcp, wout, row(ln1_g), row(ln1_b),
                  tm=nbatch)

    x1, idx, gw, rank, counts = _router(x1_p, x1_s, wr_hi, wr_lo, row(b_router))
    idx3, rank3, pstart, cnt, nact, block_exp, n_blocks = _moe_plan(idx, rank, counts, n_tokens)
    xs_sorted = _dispatch(x1, idx3, rank3, pstart, cnt, nact, n_blocks)
    y_sorted = _experts(xs_sorted, w_gate_e[0], w_up_e[0], w_down_e[0], block_exp, nact)
    y_p, y_s = _combine(x1, idx3, rank3, pstart, gw, y_sorted, wgs, wus, wds, row(ln2_g), row(ln2_b), n_prompt)

    return (y_p.reshape(batch, seq, d), y_s.reshape(nbatch, 1, d),
            newk_p, newv_p, newconv_p.reshape(1, batch, CONV_WIDTH - 1, ch),
            newk_s.reshape(1, nbatch, WINDOW, N_KV_HEADS, HEAD_DIM),
            newv_s.reshape(1, nbatch, WINDOW, N_KV_HEADS, HEAD_DIM),
            newconv_s.reshape(1, nbatch, CONV_WIDTH - 1, ch))
```

```python
import functools

import jax
import jax.numpy as jnp
from jax import lax
from jax.experimental import pallas as pl
from jax.experimental.pallas import tpu as pltpu

HEAD_DIM = 64
N_HEADS = 16
N_KV_HEADS = 4
GQA_GROUP = N_HEADS // N_KV_HEADS
WINDOW = 128
ATTN_WIDTH = N_HEADS * HEAD_DIM
KV_WIDTH = N_KV_HEADS * HEAD_DIM
CONV_WIDTH = 31
N_EXPERTS = 256
TOP_K = 8
N_EXPERT_GROUPS = 8
GROUP_SIZE = N_EXPERTS // N_EXPERT_GROUPS
TOPK_GROUPS = 4
ROUTED_SCALE = 2.5
LN_EPS = 1e-5
DEPTH = 1
DEEPNORM_ALPHA = (2 * DEPTH) ** 0.25
ALIBI_SLOPES = tuple(2.0 ** (-8.0 * (h + 1.0) / N_HEADS) for h in range(N_HEADS))
ATTN_SCALE = HEAD_DIM ** -0.5

MOE_BLOCK = 128
TOKEN_TILE = 128
LANES = 128
SUBLANES = 8
VMEM_LIMIT = 56 * 1024 * 1024

BF16 = jnp.bfloat16
F32 = jnp.float32
NEG_INF = float("-inf")


def _cparams(*sem):
    return pltpu.CompilerParams(dimension_semantics=sem, vmem_limit_bytes=VMEM_LIMIT)


def _sigmoid(x):
    return 1.0 / (1.0 + jnp.exp(-x))


def _silu(x):
    return x * _sigmoid(x)


def _layer_norm(x, g, b):
    mu = jnp.mean(x, axis=-1, keepdims=True)
    xc = x - mu
    var = jnp.mean(xc * xc, axis=-1, keepdims=True)
    return xc * lax.rsqrt(var + LN_EPS) * g + b


def _inproj_kernel(x_ref, w_ref, o_ref, xb_ref):
    @pl.when(pl.program_id(1) == 0)
    def _():
        xb_ref[...] = x_ref[...].astype(BF16)

    o_ref[...] = jnp.dot(xb_ref[...], w_ref[...], preferred_element_type=F32)


def _inproj(x2d, w_bf, tm, tn):
    m, k = x2d.shape
    n = w_bf.shape[1]
    return pl.pallas_call(
        _inproj_kernel,
        grid=(m // tm, n // tn),
        in_specs=[pl.BlockSpec((tm, k), lambda i, j: (i, 0)),
                  pl.BlockSpec((k, tn), lambda i, j: (0, j))],
        out_specs=pl.BlockSpec((tm, tn), lambda i, j: (i, j)),
        out_shape=jax.ShapeDtypeStruct((m, n), F32),
        scratch_shapes=[pltpu.VMEM((tm, k), BF16)],
        compiler_params=_cparams("parallel", "arbitrary"),
        name="inproj",
    )(x2d, w_bf)


def _attn_prompt_kernel(sinks_ref, q_ref, kp_ref, kc_ref, vp_ref, vc_ref, o_ref):
    i = pl.program_id(1)
    q = q_ref[...]
    k = jnp.concatenate([kp_ref[...], kc_ref[...]], axis=0).astype(BF16)
    v = jnp.concatenate([vp_ref[...], vc_ref[...]], axis=0).astype(BF16)
    qi = lax.broadcasted_iota(jnp.int32, (WINDOW, 2 * WINDOW), 0)
    kj = lax.broadcasted_iota(jnp.int32, (WINDOW, 2 * WINDOW), 1)
    dist = qi + WINDOW - kj
    valid = (dist >= 0) & (dist <= WINDOW) & ((kj >= WINDOW) | (i > 0))
    distf = dist.astype(F32)
    outs = []
    for h in range(N_HEADS):
        kv = h // GQA_GROUP
        qh = q[:, h * HEAD_DIM:(h + 1) * HEAD_DIM].astype(BF16)
        kh = k[:, kv * HEAD_DIM:(kv + 1) * HEAD_DIM]
        vh = v[:, kv * HEAD_DIM:(kv + 1) * HEAD_DIM]
        s = lax.dot_general(qh, kh, (((1,), (1,)), ((), ())), preferred_element_type=F32) * ATTN_SCALE
        s = jnp.where(valid, s - ALIBI_SLOPES[h] * distf, NEG_INF)
        sink = sinks_ref[h]
        m = jnp.maximum(jnp.max(s, axis=-1, keepdims=True), sink)
        e = jnp.exp(s - m)
        denom = jnp.sum(e, axis=-1, keepdims=True) + jnp.exp(sink - m)
        p = (e / denom).astype(BF16)
        outs.append(jnp.dot(p, vh, preferred_element_type=F32))
    o_ref[...] = jnp.concatenate(outs, axis=-1).astype(o_ref.dtype)


def _attn_prompt(z2d, sinks, batch, seq):
    nb = seq // WINDOW
    kcol = ATTN_WIDTH // KV_WIDTH
    vcol = kcol + 1

    def cur(col):
        return lambda b, i: (b * nb + i, col)

    def prev(col):
        return lambda b, i: (b * nb + jnp.maximum(i - 1, 0), col)

    return pl.pallas_call(
        _attn_prompt_kernel,
        grid=(batch, nb),
        in_specs=[pl.BlockSpec(memory_space=pltpu.SMEM),
                  pl.BlockSpec((WINDOW, ATTN_WIDTH), cur(0)),
                  pl.BlockSpec((WINDOW, KV_WIDTH), prev(kcol)),
                  pl.BlockSpec((WINDOW, KV_WIDTH), cur(kcol)),
                  pl.BlockSpec((WINDOW, KV_WIDTH), prev(vcol)),
                  pl.BlockSpec((WINDOW, KV_WIDTH), cur(vcol))],
        out_specs=pl.BlockSpec((WINDOW, ATTN_WIDTH), cur(0)),
        out_shape=jax.ShapeDtypeStruct((batch * seq, ATTN_WIDTH), BF16),
        compiler_params=_cparams("parallel", "parallel"),
        name="attn_prompt",
    )(sinks, z2d, z2d, z2d, z2d, z2d)


def _attn_sample_kernel(sinks_ref, slopes_ref, q_ref, kn_ref, vn_ref, ck_ref, cv_ref,
                        o_ref, nk_ref, nv_ref, *, bb):
    dist_c = (WINDOW - lax.broadcasted_iota(jnp.int32, (1, WINDOW), 1)).astype(F32)
    sink = sinks_ref[...]
    bias_c = slopes_ref[...] * dist_c
    head = lax.broadcasted_iota(jnp.int32, (N_HEADS, KV_WIDTH), 0)
    lane = lax.broadcasted_iota(jnp.int32, (N_HEADS, KV_WIDTH), 1)
    own = (lane // HEAD_DIM) == (head // GQA_GROUP)
    for b in range(bb):
        qb = q_ref[b].astype(BF16)
        kc = ck_ref[b]
        vc = cv_ref[b]
        kn = kn_ref[b]
        vn = vn_ref[b]
        sc = lax.dot_general(qb, kc.astype(BF16), (((1,), (1,)), ((), ())),
                             preferred_element_type=F32) * ATTN_SCALE - bias_c
        sn = jnp.sum(qb.astype(F32) * kn.astype(BF16).astype(F32), axis=-1, keepdims=True) * ATTN_SCALE
        m = jnp.maximum(jnp.maximum(jnp.max(sc, axis=-1, keepdims=True), sn), sink)
        ec = jnp.exp(sc - m)
        en = jnp.exp(sn - m)
        denom = jnp.sum(ec, axis=-1, keepdims=True) + en + jnp.exp(sink - m)
        pc = (ec / denom).astype(BF16)
        pn = (en / denom).astype(BF16).astype(F32)
        r = jnp.dot(pc, vc.astype(BF16), preferred_element_type=F32) + pn * vn.astype(BF16).astype(F32)
        r = jnp.where(own, r, 0.0)
        o = r[:, 0:HEAD_DIM]
        for kv in range(1, N_KV_HEADS):
            o = o + r[:, kv * HEAD_DIM:(kv + 1) * HEAD_DIM]
        o_ref[b] = o.astype(o_ref.dtype)
        nk_ref[b, 0:WINDOW - 1, :] = kc[1:WINDOW, :]
        nk_ref[b, WINDOW - 1:WINDOW, :] = kn
        nv_ref[b, 0:WINDOW - 1, :] = vc[1:WINDOW, :]
        nv_ref[b, WINDOW - 1:WINDOW, :] = vn


def _attn_sample(q3, kn3, vn3, ck, cv, sinks_col, slopes_col, bb):
    nbatch = q3.shape[0]
    blk3 = lambda s1, s2: pl.BlockSpec((bb, s1, s2), lambda i: (i, 0, 0))
    full = lambda a: pl.BlockSpec(a.shape, lambda i: (0,) * a.ndim)
    return pl.pallas_call(
        functools.partial(_attn_sample_kernel, bb=bb),
        grid=(nbatch // bb,),
        in_specs=[full(sinks_col), full(slopes_col),
                  blk3(N_HEADS, KV_WIDTH), blk3(1, KV_WIDTH), blk3(1, KV_WIDTH),
                  blk3(WINDOW, KV_WIDTH), blk3(WINDOW, KV_WIDTH)],
        out_specs=[blk3(N_HEADS, HEAD_DIM), blk3(WINDOW, KV_WIDTH), blk3(WINDOW, KV_WIDTH)],
        out_shape=[jax.ShapeDtypeStruct((nbatch, N_HEADS, HEAD_DIM), BF16),
                   jax.ShapeDtypeStruct((nbatch, WINDOW, KV_WIDTH), F32),
                   jax.ShapeDtypeStruct((nbatch, WINDOW, KV_WIDTH), F32)],
        compiler_params=_cparams("parallel"),
        name="attn_sample",
    )(sinks_col, slopes_col, q3, kn3, vn3, ck, cv)


CONV_HALO = 32
CONV_ROWS = 64


def _conv_prompt_kernel(ca0, ca1, cb0, cb1, w_ref, b_ref, g_ref, beta_ref,
                        c_ref, nc_ref, ext, ybuf, *, ts):
    i = pl.program_id(1)
    half = ca0.shape[1]

    @pl.when(i == 0)
    def _():
        ext[0:CONV_HALO, :] = jnp.zeros((CONV_HALO, ext.shape[1]), F32)

    @pl.when(i > 0)
    def _():
        ext[0:CONV_HALO, :] = ext[ts:ts + CONV_HALO, :]

    ext[CONV_HALO:CONV_HALO + ts, 0:half] = ca0[...] * _sigmoid(cb0[...])
    ext[CONV_HALO:CONV_HALO + ts, half:2 * half] = ca1[...] * _sigmoid(cb1[...])

    shift = CONV_HALO - (CONV_WIDTH - 1)
    for r in range(ts // CONV_ROWS):
        for c in range(ext.shape[1] // LANES):
            cols = slice(c * LANES, (c + 1) * LANES)
            acc = jnp.zeros((CONV_ROWS, LANES), F32)
            for j in range(CONV_WIDTH):
                r0 = r * CONV_ROWS + shift + j
                acc = acc + w_ref[j:j + 1, cols] * ext[r0:r0 + CONV_ROWS, cols]
            ybuf[r * CONV_ROWS:(r + 1) * CONV_ROWS, cols] = acc

    y = ybuf[...] + b_ref[...]
    c_ref[...] = _silu(_layer_norm(y, g_ref[...], beta_ref[...])).astype(c_ref.dtype)

    @pl.when(i == pl.num_programs(1) - 1)
    def _():
        nc_ref[...] = ext[CONV_HALO + ts - (CONV_WIDTH - 1):CONV_HALO + ts, :]


def _conv_prompt(z2d, conv_w, conv_b, ln_g, ln_b, batch, seq, ts):
    nt = seq // ts
    ch = conv_w.shape[1]
    half = ch // 2
    ca_col = (ATTN_WIDTH + 2 * KV_WIDTH) // half
    cb_col = ca_col + 2
    zblk = lambda col: pl.BlockSpec((ts, half), lambda b, i: (b * nt + i, col))
    full = lambda a: pl.BlockSpec(a.shape, lambda b, i: (0,) * a.ndim)
    return pl.pallas_call(
        functools.partial(_conv_prompt_kernel, ts=ts),
        grid=(batch, nt),
        in_specs=[zblk(ca_col), zblk(ca_col + 1), zblk(cb_col), zblk(cb_col + 1),
                  full(conv_w), full(conv_b), full(ln_g), full(ln_b)],
        out_specs=[pl.BlockSpec((ts, ch), lambda b, i: (b * nt + i, 0)),
                   pl.BlockSpec((None, CONV_WIDTH - 1, ch), lambda b, i: (b, 0, 0))],
        out_shape=[jax.ShapeDtypeStruct((batch * seq, ch), BF16),
                   jax.ShapeDtypeStruct((batch, CONV_WIDTH - 1, ch), F32)],
        scratch_shapes=[pltpu.VMEM((CONV_HALO + ts, ch), F32), pltpu.VMEM((ts, ch), F32)],
        compiler_params=_cparams("parallel", "arbitrary"),
        name="conv_prompt",
    )(z2d, z2d, z2d, z2d, conv_w, conv_b, ln_g, ln_b)


def _conv_sample_kernel(st_ref, ca_ref, cb_ref, w_ref, b_ref, g_ref, beta_ref, c_ref, ns_ref, *, bb):
    u = ca_ref[...] * _sigmoid(cb_ref[...])
    w_past = w_ref[0:CONV_WIDTH - 1, :]
    w_new = w_ref[CONV_WIDTH - 1:CONV_WIDTH, :]
    for b in range(bb):
        st = st_ref[b]
        ub = u[b:b + 1, :]
        y = jnp.sum(st * w_past, axis=0, keepdims=True) + ub * w_new + b_ref[...]
        c_ref[b:b + 1, :] = _silu(_layer_norm(y, g_ref[...], beta_ref[...])).astype(c_ref.dtype)
        ns_ref[b, 0:CONV_WIDTH - 2, :] = st[1:CONV_WIDTH - 1, :]
        ns_ref[b, CONV_WIDTH - 2:CONV_WIDTH - 1, :] = ub


def _conv_sample(state, ca, cb, conv_w, conv_b, ln_g, ln_b, bb):
    nbatch, npast, ch = state.shape
    full = lambda a: pl.BlockSpec(a.shape, lambda i: (0,) * a.ndim)
    return pl.pallas_call(
        functools.partial(_conv_sample_kernel, bb=bb),
        grid=(nbatch // bb,),
        in_specs=[pl.BlockSpec((bb, npast, ch), lambda i: (i, 0, 0)),
                  pl.BlockSpec((bb, ch), lambda i: (i, 0)),
                  pl.BlockSpec((bb, ch), lambda i: (i, 0)),
                  full(conv_w), full(conv_b), full(ln_g), full(ln_b)],
        out_specs=[pl.BlockSpec((bb, ch), lambda i: (i, 0)),
                   pl.BlockSpec((bb, npast, ch), lambda i: (i, 0, 0))],
        out_shape=[jax.ShapeDtypeStruct((nbatch, ch), BF16),
                   jax.ShapeDtypeStruct((nbatch, npast, ch), F32)],
        compiler_params=_cparams("parallel"),
        name="conv_sample",
    )(state, ca, cb, conv_w, conv_b, ln_g, ln_b)


def _merge_kernel(*refs, n_gate_blocks):
    x_ref, o_ref, c_ref = refs[0:3]
    ga_refs = refs[3:3 + n_gate_blocks]
    gc_refs = refs[3 + n_gate_blocks:3 + 2 * n_gate_blocks]
    wap_ref, wcp_ref, wout_ref, g_ref, b_ref, out_ref = refs[3 + 2 * n_gate_blocks:]
    a = jnp.dot(o_ref[...], wap_ref[...], preferred_element_type=F32)
    cc = jnp.dot(c_ref[...], wcp_ref[...], preferred_element_type=F32)
    ga = jnp.concatenate([r[...] for r in ga_refs], axis=-1)
    gc = jnp.concatenate([r[...] for r in gc_refs], axis=-1)
    mixed = _sigmoid(ga) * a + _sigmoid(gc) * cc
    y = jnp.dot(mixed.astype(BF16), wout_ref[...], preferred_element_type=F32)
    out_ref[...] = _layer_norm(DEEPNORM_ALPHA * x_ref[...] + y, g_ref[...], b_ref[...])


def _merge(x2d, o2d, c2d, z2d, wap, wcp, wout, ln_g, ln_b, tm):
    m, d = x2d.shape
    gw = 512
    n_gate_blocks = d // gw
    ga_col = (z2d.shape[1] - 2 * d) // gw
    gc_col = ga_col + n_gate_blocks
    row = lambda w: pl.BlockSpec((tm, w), lambda i: (i, 0))
    zblk = lambda col: pl.BlockSpec((tm, gw), lambda i: (i, col))
    full = lambda a: pl.BlockSpec(a.shape, lambda i: (0,) * a.ndim)
    in_specs = ([row(d), row(o2d.shape[1]), row(c2d.shape[1])]
                + [zblk(ga_col + k) for k in range(n_gate_blocks)]
                + [zblk(gc_col + k) for k in range(n_gate_blocks)]
                + [full(wap), full(wcp), full(wout), full(ln_g), full(ln_b)])
    args = [x2d, o2d, c2d] + [z2d] * (2 * n_gate_blocks) + [wap, wcp, wout, ln_g, ln_b]
    return pl.pallas_call(
        functools.partial(_merge_kernel, n_gate_blocks=n_gate_blocks),
        grid=(m // tm,),
        in_specs=in_specs,
        out_specs=pl.BlockSpec((tm, d), lambda i: (i, 0)),
        out_shape=jax.ShapeDtypeStruct((m, d), F32),
        compiler_params=_cparams("parallel"),
        name="merge",
    )(*args)


def _router_kernel(xp_ref, xs_ref, wh_ref, wl_ref, b_ref, x_ref, idx_ref, gw_ref, rank_ref, cnt_ref, carry,
                   *, n_prompt_tiles):
    @pl.when(pl.program_id(0) == 0)
    def _():
        carry[...] = jnp.zeros(carry.shape, F32)

    x = jnp.where(pl.program_id(0) < n_prompt_tiles, xp_ref[...], xs_ref[...])
    x_ref[...] = x
    tr = x.shape[0]
    xh = x.astype(BF16)
    xl = (x - xh.astype(F32)).astype(BF16)
    wh = wh_ref[...]
    logits = (jnp.dot(xh, wh, preferred_element_type=F32)
              + jnp.dot(xl, wh, preferred_element_type=F32)
              + jnp.dot(xh, wl_ref[...], preferred_element_type=F32))
    scores = _sigmoid(logits)
    sel = scores + b_ref[...]
    lane = lax.broadcasted_iota(jnp.int32, (tr, N_EXPERTS), 1)
    lane_f = lane.astype(F32)
    grp = lane // GROUP_SIZE

    gscore = []
    for g in range(N_EXPERT_GROUPS):
        v = jnp.where(grp == g, sel, NEG_INF)
        m1 = jnp.max(v, axis=-1, keepdims=True)
        is_max = v == m1
        n_max = jnp.sum(is_max.astype(F32), axis=-1, keepdims=True)
        below = jnp.max(jnp.where(is_max, NEG_INF, v), axis=-1, keepdims=True)
        gscore.append(m1 + jnp.where(n_max > 1.5, m1, below))

    keep = jnp.zeros((tr, N_EXPERTS), jnp.bool_)
    for g in range(N_EXPERT_GROUPS):
        beaten = jnp.zeros((tr, 1), jnp.int32)
        for o in range(N_EXPERT_GROUPS):
            if o == g:
                continue
            wins = (gscore[o] > gscore[g]) | ((gscore[o] == gscore[g]) & (o < g))
            beaten = beaten + wins.astype(jnp.int32)
        keep = keep | ((beaten < TOPK_GROUPS) & (grp == g))

    cur = jnp.where(keep, sel, NEG_INF)
    picked = []
    weights = []
    onehot = jnp.zeros((tr, N_EXPERTS), F32)
    for _ in range(TOP_K):
        m = jnp.max(cur, axis=-1, keepdims=True)
        ik = jnp.min(jnp.where(cur == m, lane_f, float(N_EXPERTS)), axis=-1, keepdims=True)
        hit = lane_f == ik
        weights.append(jnp.sum(jnp.where(hit, scores, 0.0), axis=-1, keepdims=True))
        cur = jnp.where(hit, NEG_INF, cur)
        onehot = onehot + hit.astype(F32)
        picked.append(ik)
    wsum = weights[0]
    for w in weights[1:]:
        wsum = wsum + w

    ri = lax.broadcasted_iota(jnp.int32, (tr, tr), 0)
    ci = lax.broadcasted_iota(jnp.int32, (tr, tr), 1)
    tri = (ri > ci).astype(BF16)
    before = jnp.dot(tri, onehot.astype(BF16), preferred_element_type=F32) + carry[...]
    carry[...] = carry[...] + jnp.sum(onehot, axis=0, keepdims=True)
    cnt_ref[...] = carry[...]

    slot = lax.broadcasted_iota(jnp.int32, (tr, LANES), 1)
    idx_out = jnp.zeros((tr, LANES), jnp.int32)
    rank_out = jnp.zeros((tr, LANES), jnp.int32)
    gw_out = jnp.zeros((tr, LANES), F32)
    for k in range(TOP_K):
        rk = jnp.sum(jnp.where(lane_f == picked[k], before, 0.0), axis=-1, keepdims=True)
        idx_out = jnp.where(slot == k, picked[k].astype(jnp.int32), idx_out)
        rank_out = jnp.where(slot == k, rk.astype(jnp.int32), rank_out)
        gw_out = jnp.where(slot == k, weights[k] / wsum * ROUTED_SCALE, gw_out)
    idx_ref[...] = idx_out
    rank_ref[...] = rank_out
    gw_ref[...] = gw_out


def _router(x1_prompt, x1_sample, wr_hi, wr_lo, b_router):
    d = x1_prompt.shape[1]
    tr = TOKEN_TILE
    npt = x1_prompt.shape[0] // tr
    t = x1_prompt.shape[0] + x1_sample.shape[0]
    full = lambda a: pl.BlockSpec(a.shape, lambda i: (0,) * a.ndim)
    tile = pl.BlockSpec((tr, LANES), lambda i: (i, 0))
    return pl.pallas_call(
        functools.partial(_router_kernel, n_prompt_tiles=npt),
        grid=(t // tr,),
        in_specs=[pl.BlockSpec((tr, d), lambda i: (jnp.minimum(i, npt - 1), 0)),
                  pl.BlockSpec((tr, d), lambda i: (jnp.maximum(i - npt, 0), 0)),
                  full(wr_hi), full(wr_lo), full(b_router)],
        out_specs=[pl.BlockSpec((tr, d), lambda i: (i, 0)), tile, tile, tile,
                   pl.BlockSpec((1, N_EXPERTS), lambda i: (0, 0))],
        out_shape=[jax.ShapeDtypeStruct((t, d), F32),
                   jax.ShapeDtypeStruct((t, LANES), jnp.int32),
                   jax.ShapeDtypeStruct((t, LANES), F32),
                   jax.ShapeDtypeStruct((t, LANES), jnp.int32),
                   jax.ShapeDtypeStruct((1, N_EXPERTS), F32)],
        scratch_shapes=[pltpu.VMEM((1, N_EXPERTS), F32)],
        compiler_params=_cparams("arbitrary"),
        name="router",
    )(x1_prompt, x1_sample, wr_hi, wr_lo, b_router)


TOKEN_CHUNKS = 16
TOKEN_ROWS = TOKEN_CHUNKS + 2
PAD_CHUNKS = (64, 32, 16, 8, 4, 2, 1)


def _pack_rows(x, dst_ref, n_rows):
    for c in range(TOKEN_CHUNKS):
        dst_ref[pl.ds(c, n_rows, stride=TOKEN_ROWS), :] = x[:, c * LANES:(c + 1) * LANES]
    for c in range(TOKEN_CHUNKS, TOKEN_ROWS):
        dst_ref[pl.ds(c, n_rows, stride=TOKEN_ROWS), :] = jnp.zeros((n_rows, LANES), x.dtype)


def _unpack_rows(src_ref, slot0, n_rows):
    row0 = slot0 * TOKEN_ROWS
    return jnp.concatenate([src_ref[pl.ds(row0 + c, n_rows, stride=TOKEN_ROWS), :] for c in range(TOKEN_CHUNKS)],
                           axis=1)


def _slot_rows(slot, n_slots=1):
    return pl.ds(slot * TOKEN_ROWS, n_slots * TOKEN_ROWS)


def _dispatch_kernel(pstart_ref, cnt_ref, nact_ref, idx_ref, rank_ref, x_ref, xs_hbm, pbuf, zbuf, sem, zsem,
                     *, n_blocks):
    i = pl.program_id(0)
    tt = TOKEN_TILE

    def chunk_fill(slot, p):
        return pltpu.make_async_copy(zbuf.at[pl.ds(0, p * TOKEN_ROWS), :], xs_hbm.at[_slot_rows(slot, p), :], zsem)

    def block_fill(b):
        return chunk_fill(b * MOE_BLOCK, MOE_BLOCK)

    def pad_fills(e, act):
        n = cnt_ref[e]
        slot = pstart_ref[e] + n
        padn = (MOE_BLOCK - (n & (MOE_BLOCK - 1))) & (MOE_BLOCK - 1)
        for p in PAD_CHUNKS:
            @pl.when((padn & p) != 0)
            def _():
                act(chunk_fill(slot, p))
            slot = slot + (padn & p)

    @pl.when(i == 0)
    def _():
        zbuf[...] = jnp.zeros(zbuf.shape, zbuf.dtype)

        def start_e(e, c):
            pad_fills(e, lambda cp: cp.start())
            return c

        def wait_e(e, c):
            pad_fills(e, lambda cp: cp.wait())
            return c

        def start_b(b, c):
            block_fill(b).start()
            return c

        def wait_b(b, c):
            block_fill(b).wait()
            return c

        lax.fori_loop(0, N_EXPERTS, start_e, 0)
        lax.fori_loop(nact_ref[0], n_blocks, start_b, 0)
        lax.fori_loop(0, N_EXPERTS, wait_e, 0)
        lax.fori_loop(nact_ref[0], n_blocks, wait_b, 0)

    _pack_rows(x_ref[...], pbuf, tt)

    def token(t, c):
        src = pbuf.at[_slot_rows(t), :]
        for j in range(TOP_K):
            a = t * TOP_K + j
            d = pstart_ref[idx_ref[0, 0, a]] + rank_ref[0, 0, a]
            pltpu.make_async_copy(src, xs_hbm.at[_slot_rows(d), :], sem).start()
        return c

    lax.fori_loop(0, tt, token, 0)
    for j in range(TOP_K):
        pltpu.make_async_copy(pbuf, xs_hbm.at[_slot_rows(0, tt), :], sem).wait()


def _dispatch(x1, idx3, rank3, pstart, cnt, nact, n_blocks):
    t, d = x1.shape
    tt = TOKEN_TILE
    assert d == TOKEN_CHUNKS * LANES
    smem_tile = pl.BlockSpec((1, 1, tt * TOP_K), lambda i, *_: (i, 0, 0), memory_space=pltpu.SMEM)
    return pl.pallas_call(
        functools.partial(_dispatch_kernel, n_blocks=n_blocks),
        grid_spec=pltpu.PrefetchScalarGridSpec(
            num_scalar_prefetch=3,
            grid=(t // tt,),
            in_specs=[smem_tile, smem_tile, pl.BlockSpec((tt, d), lambda i, *_: (i, 0))],
            out_specs=pl.BlockSpec(memory_space=pl.ANY),
            scratch_shapes=[pltpu.VMEM((tt * TOKEN_ROWS, LANES), F32),
                            pltpu.VMEM((MOE_BLOCK * TOKEN_ROWS, LANES), F32),
                            pltpu.SemaphoreType.DMA, pltpu.SemaphoreType.DMA]),
        out_shape=jax.ShapeDtypeStruct((n_blocks * MOE_BLOCK * TOKEN_ROWS, LANES), F32),
        compiler_params=pltpu.CompilerParams(dimension_semantics=("arbitrary",),
                                             vmem_limit_bytes=VMEM_LIMIT, has_side_effects=True),
        name="dispatch",
    )(pstart, cnt, nact, idx3, rank3, x1)


def _experts_kernel(bexp_ref, nact_ref, xs_ref, wg_hbm, wu_hbm, wd_hbm, y_ref,
                    wg_f32, wu_f32, wd_f32, wg_bf, wu_bf, wd_bf, slot_ref, sems):
    b = pl.program_id(0)
    nact = nact_ref[0]
    last = pl.num_programs(0) - 1
    active = b < nact
    e = bexp_ref[b]
    new_expert = (b == 0) | (e != bexp_ref[jnp.maximum(b - 1, 0)])

    def fetch(expert, slot):
        return (pltpu.make_async_copy(wg_hbm.at[expert], wg_f32.at[slot], sems.at[slot, 0]),
                pltpu.make_async_copy(wu_hbm.at[expert], wu_f32.at[slot], sems.at[slot, 1]),
                pltpu.make_async_copy(wd_hbm.at[expert], wd_f32.at[slot], sems.at[slot, 2]))

    @pl.when(active & new_expert)
    def _():
        @pl.when(b == 0)
        def _():
            slot_ref[0] = 0
            for cp in fetch(e, 0):
                cp.start()

        @pl.when(b > 0)
        def _():
            slot_ref[0] = 1 - slot_ref[0]

        slot = slot_ref[0]
        nxt = lax.while_loop(lambda j: (j < nact) & (bexp_ref[jnp.minimum(j, last)] == e),
                             lambda j: j + 1, b + 1)

        @pl.when(nxt < nact)
        def _():
            for cp in fetch(bexp_ref[jnp.minimum(nxt, last)], 1 - slot):
                cp.start()

        cp_g, cp_u, cp_d = fetch(e, slot)
        cp_g.wait()
        wg_bf[...] = wg_f32[slot].astype(BF16)
        cp_u.wait()
        wu_bf[...] = wu_f32[slot].astype(BF16)
        cp_d.wait()
        wd_bf[...] = wd_f32[slot].astype(BF16)

    @pl.when(active)
    def _():
        x = _unpack_rows(xs_ref, 0, MOE_BLOCK).astype(BF16)
        g = jnp.dot(x, wg_bf[...], preferred_element_type=F32)
        u = jnp.dot(x, wu_bf[...], preferred_element_type=F32)
        h = (_silu(g) * u).astype(BF16)
        _pack_rows(jnp.dot(h, wd_bf[...], preferred_element_type=F32), y_ref, MOE_BLOCK)

    @pl.when(jnp.logical_not(active))
    def _():
        y_ref[...] = jnp.zeros(y_ref.shape, y_ref.dtype)


def _experts(xs, w_gate_e, w_up_e, w_down_e, block_exp, nact):
    n_blocks = xs.shape[0] // (MOE_BLOCK * TOKEN_ROWS)
    _, d, ff = w_gate_e.shape
    blk = (MOE_BLOCK * TOKEN_ROWS, LANES)
    live = lambda b, bexp, nact: (jnp.minimum(b, nact[0] - 1), 0)
    hbm = pl.BlockSpec(memory_space=pl.ANY)
    return pl.pallas_call(
        _experts_kernel,
        grid_spec=pltpu.PrefetchScalarGridSpec(
            num_scalar_prefetch=2,
            grid=(n_blocks,),
            in_specs=[pl.BlockSpec(blk, live), hbm, hbm, hbm],
            out_specs=pl.BlockSpec(blk, lambda b, bexp, nact: (b, 0)),
            scratch_shapes=[pltpu.VMEM((2, d, ff), F32), pltpu.VMEM((2, d, ff), F32), pltpu.VMEM((2, ff, d), F32),
                            pltpu.VMEM((d, ff), BF16), pltpu.VMEM((d, ff), BF16), pltpu.VMEM((ff, d), BF16),
                            pltpu.SMEM((1,), jnp.int32), pltpu.SemaphoreType.DMA((2, 3))]),
        out_shape=jax.ShapeDtypeStruct(xs.shape, F32),
        compiler_params=_cparams("arbitrary"),
        name="experts",
    )(block_exp, nact, xs, w_gate_e, w_up_e, w_down_e)


def _combine_kernel(pstart_ref, idx_ref, rank_ref, x_ref, gw_ref, y_hbm, wgs_ref, wus_ref, wds_ref, g_ref, b_ref,
                    op_ref, os_ref, ybuf, sem, *, n_prompt_tiles):
    i = pl.program_id(0)
    tt = TOKEN_TILE

    def token(t, c):
        for j in range(TOP_K):
            a = t * TOP_K + j
            d = pstart_ref[idx_ref[0, 0, a]] + rank_ref[0, 0, a]
            pltpu.make_async_copy(y_hbm.at[_slot_rows(d), :], ybuf.at[_slot_rows(j * tt + t), :], sem).start()
        return c

    lax.fori_loop(0, tt, token, 0)

    x = x_ref[...]
    xb = x.astype(BF16)
    hs = _silu(jnp.dot(xb, wgs_ref[...], preferred_element_type=F32)) * jnp.dot(
        xb, wus_ref[...], preferred_element_type=F32)
    f = jnp.dot(hs.astype(BF16), wds_ref[...], preferred_element_type=F32)

    for j in range(TOP_K):
        pltpu.make_async_copy(y_hbm.at[_slot_rows(0, tt), :], ybuf.at[_slot_rows(j * tt, tt), :], sem).wait()
    gw = gw_ref[...]
    for j in range(TOP_K):
        f = f + gw[:, j:j + 1] * _unpack_rows(ybuf, j * tt, tt)
    out = _layer_norm(DEEPNORM_ALPHA * x + f, g_ref[...], b_ref[...])

    @pl.when(i < n_prompt_tiles)
    def _():
        op_ref[...] = out

    @pl.when(i >= n_prompt_tiles)
    def _():
        os_ref[...] = out


def _combine(x1, idx3, rank3, pstart, gw, y, wgs, wus, wds, ln_g, ln_b, n_prompt_rows):
    t, d = x1.shape
    tt = TOKEN_TILE
    nt = t // tt
    npt = n_prompt_rows // tt
    full = lambda a: pl.BlockSpec(a.shape, lambda i, *_: (0,) * a.ndim)
    smem_tile = pl.BlockSpec((1, 1, tt * TOP_K), lambda i, *_: (i, 0, 0), memory_space=pltpu.SMEM)
    return pl.pallas_call(
        functools.partial(_combine_kernel, n_prompt_tiles=npt),
        grid_spec=pltpu.PrefetchScalarGridSpec(
            num_scalar_prefetch=1,
            grid=(nt,),
            in_specs=[smem_tile, smem_tile,
                      pl.BlockSpec((tt, d), lambda i, *_: (i, 0)),
                      pl.BlockSpec((tt, LANES), lambda i, *_: (i, 0)),
                      pl.BlockSpec(memory_space=pl.ANY),
                      full(wgs), full(wus), full(wds), full(ln_g), full(ln_b)],
            out_specs=[pl.BlockSpec((tt, d), lambda i, *_: (jnp.minimum(i, npt - 1), 0)),
                       pl.BlockSpec((tt, d), lambda i, *_: (jnp.maximum(i - npt, 0), 0))],
            scratch_shapes=[pltpu.VMEM((TOP_K * tt * TOKEN_ROWS, LANES), F32), pltpu.SemaphoreType.DMA]),
        out_shape=[jax.ShapeDtypeStruct((n_prompt_rows, d), F32),
                   jax.ShapeDtypeStruct((t - n_prompt_rows, d), F32)],
        compiler_params=_cparams("arbitrary"),
        name="combine",
    )(pstart, idx3, rank3, x1, gw, y, wgs, wus, wds, ln_g, ln_b)


def _moe_plan(idx, rank, counts, n_tokens):
    cnt = counts[0].astype(jnp.int32)
    padded = ((cnt + MOE_BLOCK - 1) // MOE_BLOCK) * MOE_BLOCK
    pend = jnp.cumsum(padded)
    pstart = pend - padded
    n_assign = n_tokens * TOP_K
    n_blocks = (n_assign + N_EXPERTS * (MOE_BLOCK - 1)) // MOE_BLOCK
    tile3 = lambda a: a[:, :TOP_K].reshape(n_tokens // TOKEN_TILE, 1, TOKEN_TILE * TOP_K)
    nact = (pend[-1:] // MOE_BLOCK).astype(jnp.int32)
    block_start = jnp.arange(n_blocks, dtype=jnp.int32) * MOE_BLOCK
    block_exp = jnp.minimum(jnp.sum(pend[None, :] <= block_start[:, None], axis=1), N_EXPERTS - 1)
    return tile3(idx), tile3(rank), pstart.astype(jnp.int32), cnt, nact, block_exp.astype(jnp.int32), n_blocks


def kernel(x_prompt, x_sample, cache_k, cache_v, state_conv, w_in, sinks, w_attn_proj, conv_w, conv_b,
           conv_ln_g, conv_ln_b, w_conv_proj, w_out, ln1_g, ln1_b, w_router, b_router, w_gate_e, w_up_e,
           w_down_e, w_gate_s, w_up_s, w_down_s, ln2_g, ln2_b):
    assert w_in.shape[0] == DEPTH
    batch, seq, d = x_prompt.shape
    nbatch = x_sample.shape[0]
    n_prompt = batch * seq
    n_tokens = n_prompt + nbatch
    ch = conv_w.shape[2]
    row = lambda a: a[0].reshape(1, -1)

    w_in_bf = w_in[0].astype(BF16)
    wap = w_attn_proj[0].astype(BF16)
    wcp = w_conv_proj[0].astype(BF16)
    wout = w_out[0].astype(BF16)
    wgs = w_gate_s[0].astype(BF16)
    wus = w_up_s[0].astype(BF16)
    wds = w_down_s[0].astype(BF16)
    wr = w_router[0].astype(F32)
    wr_hi = wr.astype(BF16)
    wr_lo = (wr - wr_hi.astype(F32)).astype(BF16)

    xp2d = x_prompt.reshape(n_prompt, d)
    xs2d = x_sample.reshape(nbatch, d)

    zp = _inproj(xp2d, w_in_bf, tm=1024, tn=512)
    o_p = _attn_prompt(zp, sinks[0], batch, seq)
    c_p, newconv_p = _conv_prompt(zp, conv_w[0], row(conv_b), row(conv_ln_g), row(conv_ln_b), batch, seq, ts=256)
    x1_p = _merge(xp2d, o_p, c_p, zp, wap, wcp, wout, row(ln1_g), row(ln1_b), tm=256)
    k_off, v_off, ca_off = ATTN_WIDTH, ATTN_WIDTH + KV_WIDTH, ATTN_WIDTH + 2 * KV_WIDTH
    zp3 = zp.reshape(batch, seq, -1)
    newk_p = zp3[:, seq - WINDOW:, k_off:v_off].reshape(1, batch, WINDOW, N_KV_HEADS, HEAD_DIM)
    newv_p = zp3[:, seq - WINDOW:, v_off:ca_off].reshape(1, batch, WINDOW, N_KV_HEADS, HEAD_DIM)

    zs = _inproj(xs2d, w_in_bf, tm=nbatch, tn=512)
    own_block = (jnp.arange(KV_WIDTH)[None, :] // HEAD_DIM) == (jnp.arange(N_HEADS)[:, None] // GQA_GROUP)
    q3 = jnp.where(own_block[None], jnp.tile(zs[:, :k_off].reshape(nbatch, N_HEADS, HEAD_DIM), (1, 1, N_KV_HEADS)), 0.0)
    kn3 = zs[:, k_off:v_off].reshape(nbatch, 1, KV_WIDTH)
    vn3 = zs[:, v_off:ca_off].reshape(nbatch, 1, KV_WIDTH)
    slopes_col = jnp.asarray(ALIBI_SLOPES, F32).reshape(N_HEADS, 1)
    o_s3, newk_s, newv_s = _attn_sample(
        q3, kn3, vn3, cache_k[0].reshape(nbatch, WINDOW, KV_WIDTH), cache_v[0].reshape(nbatch, WINDOW, KV_WIDTH),
        sinks[0].reshape(N_HEADS, 1), slopes_col, bb=8)
    c_s, newconv_s = _conv_sample(state_conv[0], zs[:, ca_off:ca_off + ch], zs[:, ca_off + ch:ca_off + 2 * ch],
                                  conv_w[0], row(conv_b), row(conv_ln_g), row(conv_ln_b), bb=8)
    x1_s = _merge(xs2d, o_s3.reshape(nbatch, ATTN_WIDTH), c_s, zs, wap, wcp, wout, row(ln1_g), row(ln1_b),
                  tm=nbatch)

    x1, idx, gw, rank, counts = _router(x1_p, x1_s, wr_hi, wr_lo, row(b_router))
    idx3, rank3, pstart, cnt, nact, block_exp, n_blocks = _moe_plan(idx, rank, counts, n_tokens)
    xs_sorted = _dispatch(x1, idx3, rank3, pstart, cnt, nact, n_blocks)
    y_sorted = _experts(xs_sorted, w_gate_e[0], w_up_e[0], w_down_e[0], block_exp, nact)
    y_p, y_s = _combine(x1, idx3, rank3, pstart, gw, y_sorted, wgs, wus, wds, row(ln2_g), row(ln2_b), n_prompt)

    return (y_p.reshape(batch, seq, d), y_s.reshape(nbatch, 1, d),
            newk_p, newv_p, newconv_p.reshape(1, batch, CONV_WIDTH - 1, ch),
            newk_s.reshape(1, nbatch, WINDOW, N_KV_HEADS, HEAD_DIM),
            newv_s.reshape(1, nbatch, WINDOW, N_KV_HEADS, HEAD_DIM),
            newconv_s.reshape(1, nbatch, CONV_WIDTH - 1, ch))
```

```python
import functools

import jax
import jax.numpy as jnp
from jax import lax
from jax.experimental import pallas as pl
from jax.experimental.pallas import tpu as pltpu

HEAD_DIM = 64
N_HEADS = 16
N_KV_HEADS = 4
GQA_GROUP = N_HEADS // N_KV_HEADS
WINDOW = 128
ATTN_WIDTH = N_HEADS * HEAD_DIM
KV_WIDTH = N_KV_HEADS * HEAD_DIM
CONV_WIDTH = 31
N_EXPERTS = 256
TOP_K = 8
N_EXPERT_GROUPS = 8
GROUP_SIZE = N_EXPERTS // N_EXPERT_GROUPS
TOPK_GROUPS = 4
ROUTED_SCALE = 2.5
LN_EPS = 1e-5
DEPTH = 1
DEEPNORM_ALPHA = (2 * DEPTH) ** 0.25
ALIBI_SLOPES = tuple(2.0 ** (-8.0 * (h + 1.0) / N_HEADS) for h in range(N_HEADS))
ATTN_SCALE = HEAD_DIM ** -0.5

MOE_BLOCK = 128
TOKEN_TILE = 128
LANES = 128
SUBLANES = 8
VMEM_LIMIT = 56 * 1024 * 1024

BF16 = jnp.bfloat16
F32 = jnp.float32
NEG_INF = float("-inf")


def _cparams(*sem):
    return pltpu.CompilerParams(dimension_semantics=sem, vmem_limit_bytes=VMEM_LIMIT)


def _sigmoid(x):
    return 1.0 / (1.0 + jnp.exp(-x))


def _silu(x):
    return x * _sigmoid(x)


def _layer_norm(x, g, b):
    mu = jnp.mean(x, axis=-1, keepdims=True)
    xc = x - mu
    var = jnp.mean(xc * xc, axis=-1, keepdims=True)
    return xc * lax.rsqrt(var + LN_EPS) * g + b


def _inproj_kernel(x_ref, w_ref, o_ref, xb_ref):
    @pl.when(pl.program_id(1) == 0)
    def _():
        xb_ref[...] = x_ref[...].astype(BF16)

    o_ref[...] = jnp.dot(xb_ref[...], w_ref[...], preferred_element_type=F32)


def _inproj(x2d, w_bf, tm, tn):
    m, k = x2d.shape
    n = w_bf.shape[1]
    return pl.pallas_call(
        _inproj_kernel,
        grid=(m // tm, n // tn),
        in_specs=[pl.BlockSpec((tm, k), lambda i, j: (i, 0)),
                  pl.BlockSpec((k, tn), lambda i, j: (0, j))],
        out_specs=pl.BlockSpec((tm, tn), lambda i, j: (i, j)),
        out_shape=jax.ShapeDtypeStruct((m, n), F32),
        scratch_shapes=[pltpu.VMEM((tm, k), BF16)],
        compiler_params=_cparams("parallel", "arbitrary"),
        name="inproj",
    )(x2d, w_bf)


def _attn_prompt_kernel(sinks_ref, q_ref, kp_ref, kc_ref, vp_ref, vc_ref, o_ref):
    i = pl.program_id(1)
    q = q_ref[...]
    k = jnp.concatenate([kp_ref[...], kc_ref[...]], axis=0).astype(BF16)
    v = jnp.concatenate([vp_ref[...], vc_ref[...]], axis=0).astype(BF16)
    qi = lax.broadcasted_iota(jnp.int32, (WINDOW, 2 * WINDOW), 0)
    kj = lax.broadcasted_iota(jnp.int32, (WINDOW, 2 * WINDOW), 1)
    dist = qi + WINDOW - kj
    valid = (dist >= 0) & (dist <= WINDOW) & ((kj >= WINDOW) | (i > 0))
    distf = dist.astype(F32)
    outs = []
    for h in range(N_HEADS):
        kv = h // GQA_GROUP
        qh = q[:, h * HEAD_DIM:(h + 1) * HEAD_DIM].astype(BF16)
        kh = k[:, kv * HEAD_DIM:(kv + 1) * HEAD_DIM]
        vh = v[:, kv * HEAD_DIM:(kv + 1) * HEAD_DIM]
        s = lax.dot_general(qh, kh, (((1,), (1,)), ((), ())), preferred_element_type=F32) * ATTN_SCALE
        s = jnp.where(valid, s - ALIBI_SLOPES[h] * distf, NEG_INF)
        sink = sinks_ref[h]
        m = jnp.maximum(jnp.max(s, axis=-1, keepdims=True), sink)
        e = jnp.exp(s - m)
        denom = jnp.sum(e, axis=-1, keepdims=True) + jnp.exp(sink - m)
        p = (e / denom).astype(BF16)
        outs.append(jnp.dot(p, vh, preferred_element_type=F32))
    o_ref[...] = jnp.concatenate(outs, axis=-1).astype(o_ref.dtype)


def _attn_prompt(z2d, sinks, batch, seq):
    nb = seq // WINDOW
    kcol = ATTN_WIDTH // KV_WIDTH
    vcol = kcol + 1

    def cur(col):
        return lambda b, i: (b * nb + i, col)

    def prev(col):
        return lambda b, i: (b * nb + jnp.maximum(i - 1, 0), col)

    return pl.pallas_call(
        _attn_prompt_kernel,
        grid=(batch, nb),
        in_specs=[pl.BlockSpec(memory_space=pltpu.SMEM),
                  pl.BlockSpec((WINDOW, ATTN_WIDTH), cur(0)),
                  pl.BlockSpec((WINDOW, KV_WIDTH), prev(kcol)),
                  pl.BlockSpec((WINDOW, KV_WIDTH), cur(kcol)),
                  pl.BlockSpec((WINDOW, KV_WIDTH), prev(vcol)),
                  pl.BlockSpec((WINDOW, KV_WIDTH), cur(vcol))],
        out_specs=pl.BlockSpec((WINDOW, ATTN_WIDTH), cur(0)),
        out_shape=jax.ShapeDtypeStruct((batch * seq, ATTN_WIDTH), BF16),
        compiler_params=_cparams("parallel", "parallel"),
        name="attn_prompt",
    )(sinks, z2d, z2d, z2d, z2d, z2d)


def _attn_sample_kernel(sinks_ref, slopes_ref, q_ref, kn_ref, vn_ref, ck_ref, cv_ref,
                        o_ref, nk_ref, nv_ref, *, bb):
    dist_c = (WINDOW - lax.broadcasted_iota(jnp.int32, (1, WINDOW), 1)).astype(F32)
    sink = sinks_ref[...]
    bias_c = slopes_ref[...] * dist_c
    head = lax.broadcasted_iota(jnp.int32, (N_HEADS, KV_WIDTH), 0)
    lane = lax.broadcasted_iota(jnp.int32, (N_HEADS, KV_WIDTH), 1)
    own = (lane // HEAD_DIM) == (head // GQA_GROUP)
    for b in range(bb):
        qb = q_ref[b].astype(BF16)
        kc = ck_ref[b]
        vc = cv_ref[b]
        kn = kn_ref[b]
        vn = vn_ref[b]
        sc = lax.dot_general(qb, kc.astype(BF16), (((1,), (1,)), ((), ())),
                             preferred_element_type=F32) * ATTN_SCALE - bias_c
        sn = jnp.sum(qb.astype(F32) * kn.astype(BF16).astype(F32), axis=-1, keepdims=True) * ATTN_SCALE
        m = jnp.maximum(jnp.maximum(jnp.max(sc, axis=-1, keepdims=True), sn), sink)
        ec = jnp.exp(sc - m)
        en = jnp.exp(sn - m)
        denom = jnp.sum(ec, axis=-1, keepdims=True) + en + jnp.exp(sink - m)
        pc = (ec / denom).astype(BF16)
        pn = (en / denom).astype(BF16).astype(F32)
        r = jnp.dot(pc, vc.astype(BF16), preferred_element_type=F32) + pn * vn.astype(BF16).astype(F32)
        r = jnp.where(own, r, 0.0)
        o = r[:, 0:HEAD_DIM]
        for kv in range(1, N_KV_HEADS):
            o = o + r[:, kv * HEAD_DIM:(kv + 1) * HEAD_DIM]
        o_ref[b] = o.astype(o_ref.dtype)
        nk_ref[b, 0:WINDOW - 1, :] = kc[1:WINDOW, :]
        nk_ref[b, WINDOW - 1:WINDOW, :] = kn
        nv_ref[b, 0:WINDOW - 1, :] = vc[1:WINDOW, :]
        nv_ref[b, WINDOW - 1:WINDOW, :] = vn


def _attn_sample(q3, kn3, vn3, ck, cv, sinks_col, slopes_col, bb):
    nbatch = q3.shape[0]
    blk3 = lambda s1, s2: pl.BlockSpec((bb, s1, s2), lambda i: (i, 0, 0))
    full = lambda a: pl.BlockSpec(a.shape, lambda i: (0,) * a.ndim)
    return pl.pallas_call(
        functools.partial(_attn_sample_kernel, bb=bb),
        grid=(nbatch // bb,),
        in_specs=[full(sinks_col), full(slopes_col),
                  blk3(N_HEADS, KV_WIDTH), blk3(1, KV_WIDTH), blk3(1, KV_WIDTH),
                  blk3(WINDOW, KV_WIDTH), blk3(WINDOW, KV_WIDTH)],
        out_specs=[blk3(N_HEADS, HEAD_DIM), blk3(WINDOW, KV_WIDTH), blk3(WINDOW, KV_WIDTH)],
        out_shape=[jax.ShapeDtypeStruct((nbatch, N_HEADS, HEAD_DIM), BF16),
                   jax.ShapeDtypeStruct((nbatch, WINDOW, KV_WIDTH), F32),
                   jax.ShapeDtypeStruct((nbatch, WINDOW, KV_WIDTH), F32)],
        compiler_params=_cparams("parallel"),
        name="attn_sample",
    )(sinks_col, slopes_col, q3, kn3, vn3, ck, cv)


CONV_HALO = 32
CONV_ROWS = 64


def _conv_prompt_kernel(ca0, ca1, cb0, cb1, w_ref, b_ref, g_ref, beta_ref,
                        c_ref, nc_ref, ext, shifted, ybuf, *, ts):
    i = pl.program_id(1)
    half = ca0.shape[1]
    span = shifted.shape[1]

    @pl.when(i == 0)
    def _():
        ext[0:CONV_HALO, :] = jnp.zeros((CONV_HALO, ext.shape[1]), F32)

    @pl.when(i > 0)
    def _():
        ext[0:CONV_HALO, :] = ext[ts:ts + CONV_HALO, :]

    ext[CONV_HALO:CONV_HALO + ts, 0:half] = ca0[...] * _sigmoid(cb0[...])
    ext[CONV_HALO:CONV_HALO + ts, half:2 * half] = ca1[...] * _sigmoid(cb1[...])

    for q in range(1, SUBLANES):
        shifted[q - 1] = ext[q:q + span, :]

    shift = CONV_HALO - (CONV_WIDTH - 1)
    for r in range(ts // CONV_ROWS):
        for c in range(ext.shape[1] // LANES):
            cols = slice(c * LANES, (c + 1) * LANES)
            acc = jnp.zeros((CONV_ROWS, LANES), F32)
            for j in range(CONV_WIDTH):
                q = (shift + j) % SUBLANES
                r0 = r * CONV_ROWS + shift + j - q
                rows = ext[r0:r0 + CONV_ROWS, cols] if q == 0 else shifted[q - 1, r0:r0 + CONV_ROWS, cols]
                acc = acc + w_ref[j:j + 1, cols] * rows
            ybuf[r * CONV_ROWS:(r + 1) * CONV_ROWS, cols] = acc

    y = ybuf[...] + b_ref[...]
    c_ref[...] = _silu(_layer_norm(y, g_ref[...], beta_ref[...])).astype(c_ref.dtype)

    @pl.when(i == pl.num_programs(1) - 1)
    def _():
        nc_ref[...] = ext[CONV_HALO + ts - (CONV_WIDTH - 1):CONV_HALO + ts, :]


def _conv_prompt(z2d, conv_w, conv_b, ln_g, ln_b, batch, seq, ts):
    nt = seq // ts
    ch = conv_w.shape[1]
    half = ch // 2
    ca_col = (ATTN_WIDTH + 2 * KV_WIDTH) // half
    cb_col = ca_col + 2
    zblk = lambda col: pl.BlockSpec((ts, half), lambda b, i: (b * nt + i, col))
    full = lambda a: pl.BlockSpec(a.shape, lambda b, i: (0,) * a.ndim)
    return pl.pallas_call(
        functools.partial(_conv_prompt_kernel, ts=ts),
        grid=(batch, nt),
        in_specs=[zblk(ca_col), zblk(ca_col + 1), zblk(cb_col), zblk(cb_col + 1),
                  full(conv_w), full(conv_b), full(ln_g), full(ln_b)],
        out_specs=[pl.BlockSpec((ts, ch), lambda b, i: (b * nt + i, 0)),
                   pl.BlockSpec((None, CONV_WIDTH - 1, ch), lambda b, i: (b, 0, 0))],
        out_shape=[jax.ShapeDtypeStruct((batch * seq, ch), BF16),
                   jax.ShapeDtypeStruct((batch, CONV_WIDTH - 1, ch), F32)],
        scratch_shapes=[pltpu.VMEM((CONV_HALO + ts, ch), F32),
                        pltpu.VMEM((SUBLANES - 1, CONV_HALO + ts - SUBLANES, ch), F32),
                        pltpu.VMEM((ts, ch), F32)],
        compiler_params=_cparams("parallel", "arbitrary"),
        name="conv_prompt",
    )(z2d, z2d, z2d, z2d, conv_w, conv_b, ln_g, ln_b)


def _conv_sample_kernel(st_ref, ca_ref, cb_ref, w_ref, b_ref, g_ref, beta_ref, c_ref, ns_ref, *, bb):
    u = ca_ref[...] * _sigmoid(cb_ref[...])
    w_past = w_ref[0:CONV_WIDTH - 1, :]
    w_new = w_ref[CONV_WIDTH - 1:CONV_WIDTH, :]
    for b in range(bb):
        st = st_ref[b]
        ub = u[b:b + 1, :]
        y = jnp.sum(st * w_past, axis=0, keepdims=True) + ub * w_new + b_ref[...]
        c_ref[b:b + 1, :] = _silu(_layer_norm(y, g_ref[...], beta_ref[...])).astype(c_ref.dtype)
        ns_ref[b, 0:CONV_WIDTH - 2, :] = st[1:CONV_WIDTH - 1, :]
        ns_ref[b, CONV_WIDTH - 2:CONV_WIDTH - 1, :] = ub


def _conv_sample(state, ca, cb, conv_w, conv_b, ln_g, ln_b, bb):
    nbatch, npast, ch = state.shape
    full = lambda a: pl.BlockSpec(a.shape, lambda i: (0,) * a.ndim)
    return pl.pallas_call(
        functools.partial(_conv_sample_kernel, bb=bb),
        grid=(nbatch // bb,),
        in_specs=[pl.BlockSpec((bb, npast, ch), lambda i: (i, 0, 0)),
                  pl.BlockSpec((bb, ch), lambda i: (i, 0)),
                  pl.BlockSpec((bb, ch), lambda i: (i, 0)),
                  full(conv_w), full(conv_b), full(ln_g), full(ln_b)],
        out_specs=[pl.BlockSpec((bb, ch), lambda i: (i, 0)),
                   pl.BlockSpec((bb, npast, ch), lambda i: (i, 0, 0))],
        out_shape=[jax.ShapeDtypeStruct((nbatch, ch), BF16),
                   jax.ShapeDtypeStruct((nbatch, npast, ch), F32)],
        compiler_params=_cparams("parallel"),
        name="conv_sample",
    )(state, ca, cb, conv_w, conv_b, ln_g, ln_b)


def _merge_kernel(*refs, n_gate_blocks):
    x_ref, o_ref, c_ref = refs[0:3]
    ga_refs = refs[3:3 + n_gate_blocks]
    gc_refs = refs[3 + n_gate_blocks:3 + 2 * n_gate_blocks]
    wap_ref, wcp_ref, wout_ref, g_ref, b_ref, out_ref = refs[3 + 2 * n_gate_blocks:]
    a = jnp.dot(o_ref[...], wap_ref[...], preferred_element_type=F32)
    cc = jnp.dot(c_ref[...], wcp_ref[...], preferred_element_type=F32)
    ga = jnp.concatenate([r[...] for r in ga_refs], axis=-1)
    gc = jnp.concatenate([r[...] for r in gc_refs], axis=-1)
    mixed = _sigmoid(ga) * a + _sigmoid(gc) * cc
    y = jnp.dot(mixed.astype(BF16), wout_ref[...], preferred_element_type=F32)
    out_ref[...] = _layer_norm(DEEPNORM_ALPHA * x_ref[...] + y, g_ref[...], b_ref[...])


def _merge(x2d, o2d, c2d, z2d, wap, wcp, wout, ln_g, ln_b, tm):
    m, d = x2d.shape
    gw = 512
    n_gate_blocks = d // gw
    ga_col = (z2d.shape[1] - 2 * d) // gw
    gc_col = ga_col + n_gate_blocks
    row = lambda w: pl.BlockSpec((tm, w), lambda i: (i, 0))
    zblk = lambda col: pl.BlockSpec((tm, gw), lambda i: (i, col))
    full = lambda a: pl.BlockSpec(a.shape, lambda i: (0,) * a.ndim)
    in_specs = ([row(d), row(o2d.shape[1]), row(c2d.shape[1])]
                + [zblk(ga_col + k) for k in range(n_gate_blocks)]
                + [zblk(gc_col + k) for k in range(n_gate_blocks)]
                + [full(wap), full(wcp), full(wout), full(ln_g), full(ln_b)])
    args = [x2d, o2d, c2d] + [z2d] * (2 * n_gate_blocks) + [wap, wcp, wout, ln_g, ln_b]
    return pl.pallas_call(
        functools.partial(_merge_kernel, n_gate_blocks=n_gate_blocks),
        grid=(m // tm,),
        in_specs=in_specs,
        out_specs=pl.BlockSpec((tm, d), lambda i: (i, 0)),
        out_shape=jax.ShapeDtypeStruct((m, d), F32),
        compiler_params=_cparams("parallel"),
        name="merge",
    )(*args)


def _router_kernel(xp_ref, xs_ref, wh_ref, wl_ref, b_ref, x_ref, idx_ref, gw_ref, rank_ref, cnt_ref, carry,
                   *, n_prompt_tiles):
    @pl.when(pl.program_id(0) == 0)
    def _():
        carry[...] = jnp.zeros(carry.shape, F32)

    x = jnp.where(pl.program_id(0) < n_prompt_tiles, xp_ref[...], xs_ref[...])
    x_ref[...] = x
    tr = x.shape[0]
    xh = x.astype(BF16)
    xl = (x - xh.astype(F32)).astype(BF16)
    wh = wh_ref[...]
    logits = (jnp.dot(xh, wh, preferred_element_type=F32)
              + jnp.dot(xl, wh, preferred_element_type=F32)
              + jnp.dot(xh, wl_ref[...], preferred_element_type=F32))
    scores = _sigmoid(logits)
    sel = scores + b_ref[...]
    lane = lax.broadcasted_iota(jnp.int32, (tr, N_EXPERTS), 1)
    lane_f = lane.astype(F32)
    grp = lane // GROUP_SIZE

    gscore = []
    for g in range(N_EXPERT_GROUPS):
        v = jnp.where(grp == g, sel, NEG_INF)
        m1 = jnp.max(v, axis=-1, keepdims=True)
        is_max = v == m1
        n_max = jnp.sum(is_max.astype(F32), axis=-1, keepdims=True)
        below = jnp.max(jnp.where(is_max, NEG_INF, v), axis=-1, keepdims=True)
        gscore.append(m1 + jnp.where(n_max > 1.5, m1, below))

    keep = jnp.zeros((tr, N_EXPERTS), jnp.bool_)
    for g in range(N_EXPERT_GROUPS):
        beaten = jnp.zeros((tr, 1), jnp.int32)
        for o in range(N_EXPERT_GROUPS):
            if o == g:
                continue
            wins = (gscore[o] > gscore[g]) | ((gscore[o] == gscore[g]) & (o < g))
            beaten = beaten + wins.astype(jnp.int32)
        keep = keep | ((beaten < TOPK_GROUPS) & (grp == g))

    cur = jnp.where(keep, sel, NEG_INF)
    picked = []
    weights = []
    onehot = jnp.zeros((tr, N_EXPERTS), F32)
    for _ in range(TOP_K):
        m = jnp.max(cur, axis=-1, keepdims=True)
        ik = jnp.min(jnp.where(cur == m, lane_f, float(N_EXPERTS)), axis=-1, keepdims=True)
        hit = lane_f == ik
        weights.append(jnp.sum(jnp.where(hit, scores, 0.0), axis=-1, keepdims=True))
        cur = jnp.where(hit, NEG_INF, cur)
        onehot = onehot + hit.astype(F32)
        picked.append(ik)
    wsum = weights[0]
    for w in weights[1:]:
        wsum = wsum + w

    ri = lax.broadcasted_iota(jnp.int32, (tr, tr), 0)
    ci = lax.broadcasted_iota(jnp.int32, (tr, tr), 1)
    tri = (ri > ci).astype(BF16)
    before = jnp.dot(tri, onehot.astype(BF16), preferred_element_type=F32) + carry[...]
    carry[...] = carry[...] + jnp.sum(onehot, axis=0, keepdims=True)
    cnt_ref[...] = carry[...]

    slot = lax.broadcasted_iota(jnp.int32, (tr, LANES), 1)
    idx_out = jnp.zeros((tr, LANES), jnp.int32)
    rank_out = jnp.zeros((tr, LANES), jnp.int32)
    gw_out = jnp.zeros((tr, LANES), F32)
    for k in range(TOP_K):
        rk = jnp.sum(jnp.where(lane_f == picked[k], before, 0.0), axis=-1, keepdims=True)
        idx_out = jnp.where(slot == k, picked[k].astype(jnp.int32), idx_out)
        rank_out = jnp.where(slot == k, rk.astype(jnp.int32), rank_out)
        gw_out = jnp.where(slot == k, weights[k] / wsum * ROUTED_SCALE, gw_out)
    idx_ref[...] = idx_out
    rank_ref[...] = rank_out
    gw_ref[...] = gw_out


def _router(x1_prompt, x1_sample, wr_hi, wr_lo, b_router):
    d = x1_prompt.shape[1]
    tr = TOKEN_TILE
    npt = x1_prompt.shape[0] // tr
    t = x1_prompt.shape[0] + x1_sample.shape[0]
    full = lambda a: pl.BlockSpec(a.shape, lambda i: (0,) * a.ndim)
    tile = pl.BlockSpec((tr, LANES), lambda i: (i, 0))
    return pl.pallas_call(
        functools.partial(_router_kernel, n_prompt_tiles=npt),
        grid=(t // tr,),
        in_specs=[pl.BlockSpec((tr, d), lambda i: (jnp.minimum(i, npt - 1), 0)),
                  pl.BlockSpec((tr, d), lambda i: (jnp.maximum(i - npt, 0), 0)),
                  full(wr_hi), full(wr_lo), full(b_router)],
        out_specs=[pl.BlockSpec((tr, d), lambda i: (i, 0)), tile, tile, tile,
                   pl.BlockSpec((1, N_EXPERTS), lambda i: (0, 0))],
        out_shape=[jax.ShapeDtypeStruct((t, d), F32),
                   jax.ShapeDtypeStruct((t, LANES), jnp.int32),
                   jax.ShapeDtypeStruct((t, LANES), F32),
                   jax.ShapeDtypeStruct((t, LANES), jnp.int32),
                   jax.ShapeDtypeStruct((1, N_EXPERTS), F32)],
        scratch_shapes=[pltpu.VMEM((1, N_EXPERTS), F32)],
        compiler_params=_cparams("arbitrary"),
        name="router",
    )(x1_prompt, x1_sample, wr_hi, wr_lo, b_router)


TOKEN_CHUNKS = 16
TOKEN_ROWS = TOKEN_CHUNKS + 2
PAD_CHUNKS = (64, 32, 16, 8, 4, 2, 1)


def _pack_rows(x, dst_ref, n_rows):
    for c in range(TOKEN_CHUNKS):
        dst_ref[pl.ds(c, n_rows, stride=TOKEN_ROWS), :] = x[:, c * LANES:(c + 1) * LANES]
    for c in range(TOKEN_CHUNKS, TOKEN_ROWS):
        dst_ref[pl.ds(c, n_rows, stride=TOKEN_ROWS), :] = jnp.zeros((n_rows, LANES), x.dtype)


def _unpack_rows(src_ref, slot0, n_rows):
    row0 = slot0 * TOKEN_ROWS
    return jnp.concatenate([src_ref[pl.ds(row0 + c, n_rows, stride=TOKEN_ROWS), :] for c in range(TOKEN_CHUNKS)],
                           axis=1)


def _slot_rows(slot, n_slots=1):
    return pl.ds(slot * TOKEN_ROWS, n_slots * TOKEN_ROWS)


TOKEN_WORD_ROWS = TOKEN_ROWS // 2


def _word_rows(slot, n_slots=1):
    return pl.ds(slot * TOKEN_WORD_ROWS, n_slots * TOKEN_WORD_ROWS)


def _dispatch_kernel(pstart_ref, cnt_ref, nact_ref, idx_ref, rank_ref, x_ref, xs_hbm, pbuf, zbuf, stage, sem, zsem,
                     *, n_blocks):
    i = pl.program_id(0)
    tt = TOKEN_TILE

    def chunk_fill(slot, p):
        return pltpu.make_async_copy(zbuf.at[_word_rows(0, p), :], xs_hbm.at[_word_rows(slot, p), :], zsem)

    def block_fill(b):
        return chunk_fill(b * MOE_BLOCK, MOE_BLOCK)

    def pad_fills(e, act):
        n = cnt_ref[e]
        slot = pstart_ref[e] + n
        padn = (MOE_BLOCK - (n & (MOE_BLOCK - 1))) & (MOE_BLOCK - 1)
        for p in PAD_CHUNKS:
            @pl.when((padn & p) != 0)
            def _():
                act(chunk_fill(slot, p))
            slot = slot + (padn & p)

    @pl.when(i == 0)
    def _():
        zbuf[...] = jnp.zeros(zbuf.shape, zbuf.dtype)

        def start_e(e, c):
            pad_fills(e, lambda cp: cp.start())
            return c

        def wait_e(e, c):
            pad_fills(e, lambda cp: cp.wait())
            return c

        def start_b(b, c):
            block_fill(b).start()
            return c

        def wait_b(b, c):
            block_fill(b).wait()
            return c

        lax.fori_loop(0, N_EXPERTS, start_e, 0)
        lax.fori_loop(nact_ref[0], n_blocks, start_b, 0)
        lax.fori_loop(0, N_EXPERTS, wait_e, 0)
        lax.fori_loop(nact_ref[0], n_blocks, wait_b, 0)

    _pack_rows(x_ref[...], stage, tt)
    pbuf[...] = pltpu.bitcast(stage[...].astype(BF16), jnp.uint32)

    def token(t, c):
        src = pbuf.at[_word_rows(t), :]
        for j in range(TOP_K):
            a = t * TOP_K + j
            d = pstart_ref[idx_ref[0, 0, a]] + rank_ref[0, 0, a]
            pltpu.make_async_copy(src, xs_hbm.at[_word_rows(d), :], sem).start()
        return c

    lax.fori_loop(0, tt, token, 0)
    for j in range(TOP_K):
        pltpu.make_async_copy(pbuf, xs_hbm.at[_word_rows(0, tt), :], sem).wait()


def _dispatch(x1, idx3, rank3, pstart, cnt, nact, n_blocks):
    t, d = x1.shape
    tt = TOKEN_TILE
    assert d == TOKEN_CHUNKS * LANES
    smem_tile = pl.BlockSpec((1, 1, tt * TOP_K), lambda i, *_: (i, 0, 0), memory_space=pltpu.SMEM)
    return pl.pallas_call(
        functools.partial(_dispatch_kernel, n_blocks=n_blocks),
        grid_spec=pltpu.PrefetchScalarGridSpec(
            num_scalar_prefetch=3,
            grid=(t // tt,),
            in_specs=[smem_tile, smem_tile, pl.BlockSpec((tt, d), lambda i, *_: (i, 0))],
            out_specs=pl.BlockSpec(memory_space=pl.ANY),
            scratch_shapes=[pltpu.VMEM((tt * TOKEN_WORD_ROWS, LANES), jnp.uint32),
                            pltpu.VMEM((MOE_BLOCK * TOKEN_WORD_ROWS, LANES), jnp.uint32),
                            pltpu.VMEM((tt * TOKEN_ROWS, LANES), F32),
                            pltpu.SemaphoreType.DMA, pltpu.SemaphoreType.DMA]),
        out_shape=jax.ShapeDtypeStruct((n_blocks * MOE_BLOCK * TOKEN_WORD_ROWS, LANES), jnp.uint32),
        compiler_params=pltpu.CompilerParams(dimension_semantics=("arbitrary",),
                                             vmem_limit_bytes=VMEM_LIMIT, has_side_effects=True),
        name="dispatch",
    )(pstart, cnt, nact, idx3, rank3, x1)


def _experts_kernel(bexp_ref, nact_ref, xs_ref, wg_hbm, wu_hbm, wd_hbm, y_ref,
                    wg_f32, wu_f32, wd_f32, wg_bf, wu_bf, wd_bf, stage, slot_ref, sems):
    b = pl.program_id(0)
    nact = nact_ref[0]
    last = pl.num_programs(0) - 1
    active = b < nact
    e = bexp_ref[b]
    new_expert = (b == 0) | (e != bexp_ref[jnp.maximum(b - 1, 0)])

    def fetch(expert, slot):
        return (pltpu.make_async_copy(wg_hbm.at[expert], wg_f32.at[slot], sems.at[slot, 0]),
                pltpu.make_async_copy(wu_hbm.at[expert], wu_f32.at[slot], sems.at[slot, 1]),
                pltpu.make_async_copy(wd_hbm.at[expert], wd_f32.at[slot], sems.at[slot, 2]))

    @pl.when(active & new_expert)
    def _():
        @pl.when(b == 0)
        def _():
            slot_ref[0] = 0
            for cp in fetch(e, 0):
                cp.start()

        @pl.when(b > 0)
        def _():
            slot_ref[0] = 1 - slot_ref[0]

        slot = slot_ref[0]
        nxt = lax.while_loop(lambda j: (j < nact) & (bexp_ref[jnp.minimum(j, last)] == e),
                             lambda j: j + 1, b + 1)

        @pl.when(nxt < nact)
        def _():
            for cp in fetch(bexp_ref[jnp.minimum(nxt, last)], 1 - slot):
                cp.start()

        cp_g, cp_u, cp_d = fetch(e, slot)
        cp_g.wait()
        wg_bf[...] = wg_f32[slot].astype(BF16)
        cp_u.wait()
        wu_bf[...] = wu_f32[slot].astype(BF16)
        cp_d.wait()
        wd_bf[...] = wd_f32[slot].astype(BF16)

    @pl.when(active)
    def _():
        stage[...] = pltpu.bitcast(xs_ref[...], BF16).astype(F32)
        x = _unpack_rows(stage, 0, MOE_BLOCK).astype(BF16)
        g = jnp.dot(x, wg_bf[...], preferred_element_type=F32)
        u = jnp.dot(x, wu_bf[...], preferred_element_type=F32)
        h = (_silu(g) * u).astype(BF16)
        _pack_rows(jnp.dot(h, wd_bf[...], preferred_element_type=F32), y_ref, MOE_BLOCK)

    @pl.when(jnp.logical_not(active))
    def _():
        y_ref[...] = jnp.zeros(y_ref.shape, y_ref.dtype)


def _experts(xs, w_gate_e, w_up_e, w_down_e, block_exp, nact):
    n_blocks = xs.shape[0] // (MOE_BLOCK * TOKEN_WORD_ROWS)
    _, d, ff = w_gate_e.shape
    live = lambda b, bexp, nact: (jnp.minimum(b, nact[0] - 1), 0)
    hbm = pl.BlockSpec(memory_space=pl.ANY)
    return pl.pallas_call(
        _experts_kernel,
        grid_spec=pltpu.PrefetchScalarGridSpec(
            num_scalar_prefetch=2,
            grid=(n_blocks,),
            in_specs=[pl.BlockSpec((MOE_BLOCK * TOKEN_WORD_ROWS, LANES), live), hbm, hbm, hbm],
            out_specs=pl.BlockSpec((MOE_BLOCK * TOKEN_ROWS, LANES), lambda b, bexp, nact: (b, 0)),
            scratch_shapes=[pltpu.VMEM((2, d, ff), F32), pltpu.VMEM((2, d, ff), F32), pltpu.VMEM((2, ff, d), F32),
                            pltpu.VMEM((d, ff), BF16), pltpu.VMEM((d, ff), BF16), pltpu.VMEM((ff, d), BF16),
                            pltpu.VMEM((MOE_BLOCK * TOKEN_ROWS, LANES), F32),
                            pltpu.SMEM((1,), jnp.int32), pltpu.SemaphoreType.DMA((2, 3))]),
        out_shape=jax.ShapeDtypeStruct((n_blocks * MOE_BLOCK * TOKEN_ROWS, LANES), F32),
        compiler_params=_cparams("arbitrary"),
        name="experts",
    )(block_exp, nact, xs, w_gate_e, w_up_e, w_down_e)


def _combine_kernel(pstart_ref, idx_ref, rank_ref, x_ref, gw_ref, y_hbm, wgs_ref, wus_ref, wds_ref, g_ref, b_ref,
                    op_ref, os_ref, ybuf, sem, *, n_prompt_tiles):
    i = pl.program_id(0)
    tt = TOKEN_TILE

    def token(t, c):
        for j in range(TOP_K):
            a = t * TOP_K + j
            d = pstart_ref[idx_ref[0, 0, a]] + rank_ref[0, 0, a]
            pltpu.make_async_copy(y_hbm.at[_slot_rows(d), :], ybuf.at[_slot_rows(j * tt + t), :], sem).start()
        return c

    lax.fori_loop(0, tt, token, 0)

    x = x_ref[...]
    xb = x.astype(BF16)
    hs = _silu(jnp.dot(xb, wgs_ref[...], preferred_element_type=F32)) * jnp.dot(
        xb, wus_ref[...], preferred_element_type=F32)
    f = jnp.dot(hs.astype(BF16), wds_ref[...], preferred_element_type=F32)

    for j in range(TOP_K):
        pltpu.make_async_copy(y_hbm.at[_slot_rows(0, tt), :], ybuf.at[_slot_rows(j * tt, tt), :], sem).wait()
    gw = gw_ref[...]
    for j in range(TOP_K):
        f = f + gw[:, j:j + 1] * _unpack_rows(ybuf, j * tt, tt)
    out = _layer_norm(DEEPNORM_ALPHA * x + f, g_ref[...], b_ref[...])

    @pl.when(i < n_prompt_tiles)
    def _():
        op_ref[...] = out

    @pl.when(i >= n_prompt_tiles)
    def _():
        os_ref[...] = out


def _combine(x1, idx3, rank3, pstart, gw, y, wgs, wus, wds, ln_g, ln_b, n_prompt_rows):
    t, d = x1.shape
    tt = TOKEN_TILE
    nt = t // tt
    npt = n_prompt_rows // tt
    full = lambda a: pl.BlockSpec(a.shape, lambda i, *_: (0,) * a.ndim)
    smem_tile = pl.BlockSpec((1, 1, tt * TOP_K), lambda i, *_: (i, 0, 0), memory_space=pltpu.SMEM)
    return pl.pallas_call(
        functools.partial(_combine_kernel, n_prompt_tiles=npt),
        grid_spec=pltpu.PrefetchScalarGridSpec(
            num_scalar_prefetch=1,
            grid=(nt,),
            in_specs=[smem_tile, smem_tile,
                      pl.BlockSpec((tt, d), lambda i, *_: (i, 0)),
                      pl.BlockSpec((tt, LANES), lambda i, *_: (i, 0)),
                      pl.BlockSpec(memory_space=pl.ANY),
                      full(wgs), full(wus), full(wds), full(ln_g), full(ln_b)],
            out_specs=[pl.BlockSpec((tt, d), lambda i, *_: (jnp.minimum(i, npt - 1), 0)),
                       pl.BlockSpec((tt, d), lambda i, *_: (jnp.maximum(i - npt, 0), 0))],
            scratch_shapes=[pltpu.VMEM((TOP_K * tt * TOKEN_ROWS, LANES), F32), pltpu.SemaphoreType.DMA]),
        out_shape=[jax.ShapeDtypeStruct((n_prompt_rows, d), F32),
                   jax.ShapeDtypeStruct((t - n_prompt_rows, d), F32)],
        compiler_params=_cparams("arbitrary"),
        name="combine",
    )(pstart, idx3, rank3, x1, gw, y, wgs, wus, wds, ln_g, ln_b)


def _moe_plan(idx, rank, counts, n_tokens):
    cnt = counts[0].astype(jnp.int32)
    padded = ((cnt + MOE_BLOCK - 1) // MOE_BLOCK) * MOE_BLOCK
    pend = jnp.cumsum(padded)
    pstart = pend - padded
    n_assign = n_tokens * TOP_K
    n_blocks = (n_assign + N_EXPERTS * (MOE_BLOCK - 1)) // MOE_BLOCK
    tile3 = lambda a: a[:, :TOP_K].reshape(n_tokens // TOKEN_TILE, 1, TOKEN_TILE * TOP_K)
    nact = (pend[-1:] // MOE_BLOCK).astype(jnp.int32)
    block_start = jnp.arange(n_blocks, dtype=jnp.int32) * MOE_BLOCK
    block_exp = jnp.minimum(jnp.sum(pend[None, :] <= block_start[:, None], axis=1), N_EXPERTS - 1)
    return tile3(idx), tile3(rank), pstart.astype(jnp.int32), cnt, nact, block_exp.astype(jnp.int32), n_blocks


def kernel(x_prompt, x_sample, cache_k, cache_v, state_conv, w_in, sinks, w_attn_proj, conv_w, conv_b,
           conv_ln_g, conv_ln_b, w_conv_proj, w_out, ln1_g, ln1_b, w_router, b_router, w_gate_e, w_up_e,
           w_down_e, w_gate_s, w_up_s, w_down_s, ln2_g, ln2_b):
    assert w_in.shape[0] == DEPTH
    batch, seq, d = x_prompt.shape
    nbatch = x_sample.shape[0]
    n_prompt = batch * seq
    n_tokens = n_prompt + nbatch
    ch = conv_w.shape[2]
    row = lambda a: a[0].reshape(1, -1)

    w_in_bf = w_in[0].astype(BF16)
    wap = w_attn_proj[0].astype(BF16)
    wcp = w_conv_proj[0].astype(BF16)
    wout = w_out[0].astype(BF16)
    wgs = w_gate_s[0].astype(BF16)
    wus = w_up_s[0].astype(BF16)
    wds = w_down_s[0].astype(BF16)
    wr = w_router[0].astype(F32)
    wr_hi = wr.astype(BF16)
    wr_lo = (wr - wr_hi.astype(F32)).astype(BF16)

    xp2d = x_prompt.reshape(n_prompt, d)
    xs2d = x_sample.reshape(nbatch, d)

    zp = _inproj(xp2d, w_in_bf, tm=1024, tn=1536)
    o_p = _attn_prompt(zp, sinks[0], batch, seq)
    c_p, newconv_p = _conv_prompt(zp, conv_w[0], row(conv_b), row(conv_ln_g), row(conv_ln_b), batch, seq, ts=256)
    x1_p = _merge(xp2d, o_p, c_p, zp, wap, wcp, wout, row(ln1_g), row(ln1_b), tm=256)
    k_off, v_off, ca_off = ATTN_WIDTH, ATTN_WIDTH + KV_WIDTH, ATTN_WIDTH + 2 * KV_WIDTH
    zp3 = zp.reshape(batch, seq, -1)
    newk_p = zp3[:, seq - WINDOW:, k_off:v_off].reshape(1, batch, WINDOW, N_KV_HEADS, HEAD_DIM)
    newv_p = zp3[:, seq - WINDOW:, v_off:ca_off].reshape(1, batch, WINDOW, N_KV_HEADS, HEAD_DIM)

    zs = _inproj(xs2d, w_in_bf, tm=nbatch, tn=1536)
    own_block = (jnp.arange(KV_WIDTH)[None, :] // HEAD_DIM) == (jnp.arange(N_HEADS)[:, None] // GQA_GROUP)
    q3 = jnp.where(own_block[None], jnp.tile(zs[:, :k_off].reshape(nbatch, N_HEADS, HEAD_DIM), (1, 1, N_KV_HEADS)), 0.0)
    kn3 = zs[:, k_off:v_off].reshape(nbatch, 1, KV_WIDTH)
    vn3 = zs[:, v_off:ca_off].reshape(nbatch, 1, KV_WIDTH)
    slopes_col = jnp.asarray(ALIBI_SLOPES, F32).reshape(N_HEADS, 1)
    o_s3, newk_s, newv_s = _attn_sample(
        q3, kn3, vn3, cache_k[0].reshape(nbatch, WINDOW, KV_WIDTH), cache_v[0].reshape(nbatch, WINDOW, KV_WIDTH),
        sinks[0].reshape(N_HEADS, 1), slopes_col, bb=8)
    c_s, newconv_s = _conv_sample(state_conv[0], zs[:, ca_off:ca_off + ch], zs[:, ca_off + ch:ca_off + 2 * ch],
                                  conv_w[0], row(conv_b), row(conv_ln_g), row(conv_ln_b), bb=8)
    x1_s = _merge(xs2d, o_s3.reshape(nbatch, ATTN_WIDTH), c_s, zs, wap, wcp, wout, row(ln1_g), row(ln1_b),
                  tm=nbatch)

    x1, idx, gw, rank, counts = _router(x1_p, x1_s, wr_hi, wr_lo, row(b_router))
    idx3, rank3, pstart, cnt, nact, block_exp, n_blocks = _moe_plan(idx, rank, counts, n_tokens)
    xs_sorted = _dispatch(x1, idx3, rank3, pstart, cnt, nact, n_blocks)
    y_sorted = _experts(xs_sorted, w_gate_e[0], w_up_e[0], w_down_e[0], block_exp, nact)
    y_p, y_s = _combine(x1, idx3, rank3, pstart, gw, y_sorted, wgs, wus, wds, row(ln2_g), row(ln2_b), n_prompt)

    return (y_p.reshape(batch, seq, d), y_s.reshape(nbatch, 1, d),
            newk_p, newv_p, newconv_p.reshape(1, batch, CONV_WIDTH - 1, ch),
            newk_s.reshape(1, nbatch, WINDOW, N_KV_HEADS, HEAD_DIM),
            newv_s.reshape(1, nbatch, WINDOW, N_KV_HEADS, HEAD_DIM),
            newconv_s.reshape(1, nbatch, CONV_WIDTH - 1, ch))
```

```python
import functools

import jax
import jax.numpy as jnp
from jax import lax
from jax.experimental import pallas as pl
from jax.experimental.pallas import tpu as pltpu

HEAD_DIM = 64
N_HEADS = 16
N_KV_HEADS = 4
GQA_GROUP = N_HEADS // N_KV_HEADS
WINDOW = 128
ATTN_WIDTH = N_HEADS * HEAD_DIM
KV_WIDTH = N_KV_HEADS * HEAD_DIM
CONV_WIDTH = 31
N_EXPERTS = 256
TOP_K = 8
N_EXPERT_GROUPS = 8
GROUP_SIZE = N_EXPERTS // N_EXPERT_GROUPS
TOPK_GROUPS = 4
ROUTED_SCALE = 2.5
LN_EPS = 1e-5
DEPTH = 1
DEEPNORM_ALPHA = (2 * DEPTH) ** 0.25
ALIBI_SLOPES = tuple(2.0 ** (-8.0 * (h + 1.0) / N_HEADS) for h in range(N_HEADS))
ATTN_SCALE = HEAD_DIM ** -0.5

MOE_BLOCK = 128
TOKEN_TILE = 128
LANES = 128
SUBLANES = 8
VMEM_LIMIT = 56 * 1024 * 1024

BF16 = jnp.bfloat16
F32 = jnp.float32
NEG_INF = float("-inf")


def _cparams(*sem):
    return pltpu.CompilerParams(dimension_semantics=sem, vmem_limit_bytes=VMEM_LIMIT)


def _sigmoid(x):
    return 1.0 / (1.0 + jnp.exp(-x))


def _silu(x):
    return x * _sigmoid(x)


def _layer_norm(x, g, b):
    mu = jnp.mean(x, axis=-1, keepdims=True)
    xc = x - mu
    var = jnp.mean(xc * xc, axis=-1, keepdims=True)
    return xc * lax.rsqrt(var + LN_EPS) * g + b


def _inproj_kernel(x_ref, w_ref, o_ref, xb_ref):
    @pl.when(pl.program_id(1) == 0)
    def _():
        xb_ref[...] = x_ref[...].astype(BF16)

    o_ref[...] = jnp.dot(xb_ref[...], w_ref[...], preferred_element_type=F32)


def _inproj(x2d, w_bf, tm, tn):
    m, k = x2d.shape
    n = w_bf.shape[1]
    return pl.pallas_call(
        _inproj_kernel,
        grid=(m // tm, n // tn),
        in_specs=[pl.BlockSpec((tm, k), lambda i, j: (i, 0)),
                  pl.BlockSpec((k, tn), lambda i, j: (0, j))],
        out_specs=pl.BlockSpec((tm, tn), lambda i, j: (i, j)),
        out_shape=jax.ShapeDtypeStruct((m, n), F32),
        scratch_shapes=[pltpu.VMEM((tm, k), BF16)],
        compiler_params=_cparams("parallel", "arbitrary"),
        name="inproj",
    )(x2d, w_bf)


def _attn_prompt_kernel(sinks_ref, q_ref, kp_ref, kc_ref, vp_ref, vc_ref, o_ref):
    i = pl.program_id(1)
    q = q_ref[...]
    k = jnp.concatenate([kp_ref[...], kc_ref[...]], axis=0).astype(BF16)
    v = jnp.concatenate([vp_ref[...], vc_ref[...]], axis=0).astype(BF16)
    qi = lax.broadcasted_iota(jnp.int32, (WINDOW, 2 * WINDOW), 0)
    kj = lax.broadcasted_iota(jnp.int32, (WINDOW, 2 * WINDOW), 1)
    dist = qi + WINDOW - kj
    valid = (dist >= 0) & (dist <= WINDOW) & ((kj >= WINDOW) | (i > 0))
    distf = dist.astype(F32)
    outs = []
    for h in range(N_HEADS):
        kv = h // GQA_GROUP
        qh = q[:, h * HEAD_DIM:(h + 1) * HEAD_DIM].astype(BF16)
        kh = k[:, kv * HEAD_DIM:(kv + 1) * HEAD_DIM]
        vh = v[:, kv * HEAD_DIM:(kv + 1) * HEAD_DIM]
        s = lax.dot_general(qh, kh, (((1,), (1,)), ((), ())), preferred_element_type=F32) * ATTN_SCALE
        s = jnp.where(valid, s - ALIBI_SLOPES[h] * distf, NEG_INF)
        sink = sinks_ref[h]
        m = jnp.maximum(jnp.max(s, axis=-1, keepdims=True), sink)
        e = jnp.exp(s - m)
        denom = jnp.sum(e, axis=-1, keepdims=True) + jnp.exp(sink - m)
        p = (e / denom).astype(BF16)
        outs.append(jnp.dot(p, vh, preferred_element_type=F32))
    o_ref[...] = jnp.concatenate(outs, axis=-1).astype(o_ref.dtype)


def _attn_prompt(z2d, sinks, batch, seq):
    nb = seq // WINDOW
    kcol = ATTN_WIDTH // KV_WIDTH
    vcol = kcol + 1

    def cur(col):
        return lambda b, i: (b * nb + i, col)

    def prev(col):
        return lambda b, i: (b * nb + jnp.maximum(i - 1, 0), col)

    return pl.pallas_call(
        _attn_prompt_kernel,
        grid=(batch, nb),
        in_specs=[pl.BlockSpec(memory_space=pltpu.SMEM),
                  pl.BlockSpec((WINDOW, ATTN_WIDTH), cur(0)),
                  pl.BlockSpec((WINDOW, KV_WIDTH), prev(kcol)),
                  pl.BlockSpec((WINDOW, KV_WIDTH), cur(kcol)),
                  pl.BlockSpec((WINDOW, KV_WIDTH), prev(vcol)),
                  pl.BlockSpec((WINDOW, KV_WIDTH), cur(vcol))],
        out_specs=pl.BlockSpec((WINDOW, ATTN_WIDTH), cur(0)),
        out_shape=jax.ShapeDtypeStruct((batch * seq, ATTN_WIDTH), BF16),
        compiler_params=_cparams("parallel", "parallel"),
        name="attn_prompt",
    )(sinks, z2d, z2d, z2d, z2d, z2d)


def _attn_sample_kernel(sinks_ref, slopes_ref, q_ref, kn_ref, vn_ref, ck_ref, cv_ref,
                        o_ref, nk_ref, nv_ref, *, bb):
    dist_c = (WINDOW - lax.broadcasted_iota(jnp.int32, (1, WINDOW), 1)).astype(F32)
    sink = sinks_ref[...]
    bias_c = slopes_ref[...] * dist_c
    head = lax.broadcasted_iota(jnp.int32, (N_HEADS, KV_WIDTH), 0)
    lane = lax.broadcasted_iota(jnp.int32, (N_HEADS, KV_WIDTH), 1)
    own = (lane // HEAD_DIM) == (head // GQA_GROUP)
    for b in range(bb):
        qb = q_ref[b].astype(BF16)
        kc = ck_ref[b]
        vc = cv_ref[b]
        kn = kn_ref[b]
        vn = vn_ref[b]
        sc = lax.dot_general(qb, kc.astype(BF16), (((1,), (1,)), ((), ())),
                             preferred_element_type=F32) * ATTN_SCALE - bias_c
        sn = jnp.sum(qb.astype(F32) * kn.astype(BF16).astype(F32), axis=-1, keepdims=True) * ATTN_SCALE
        m = jnp.maximum(jnp.maximum(jnp.max(sc, axis=-1, keepdims=True), sn), sink)
        ec = jnp.exp(sc - m)
        en = jnp.exp(sn - m)
        denom = jnp.sum(ec, axis=-1, keepdims=True) + en + jnp.exp(sink - m)
        pc = (ec / denom).astype(BF16)
        pn = (en / denom).astype(BF16).astype(F32)
        r = jnp.dot(pc, vc.astype(BF16), preferred_element_type=F32) + pn * vn.astype(BF16).astype(F32)
        r = jnp.where(own, r, 0.0)
        o = r[:, 0:HEAD_DIM]
        for kv in range(1, N_KV_HEADS):
            o = o + r[:, kv * HEAD_DIM:(kv + 1) * HEAD_DIM]
        o_ref[b] = o.astype(o_ref.dtype)
        nk_ref[b, 0:WINDOW - 1, :] = kc[1:WINDOW, :]
        nk_ref[b, WINDOW - 1:WINDOW, :] = kn
        nv_ref[b, 0:WINDOW - 1, :] = vc[1:WINDOW, :]
        nv_ref[b, WINDOW - 1:WINDOW, :] = vn


def _attn_sample(q3, kn3, vn3, ck, cv, sinks_col, slopes_col, bb):
    nbatch = q3.shape[0]
    blk3 = lambda s1, s2: pl.BlockSpec((bb, s1, s2), lambda i: (i, 0, 0))
    full = lambda a: pl.BlockSpec(a.shape, lambda i: (0,) * a.ndim)
    return pl.pallas_call(
        functools.partial(_attn_sample_kernel, bb=bb),
        grid=(nbatch // bb,),
        in_specs=[full(sinks_col), full(slopes_col),
                  blk3(N_HEADS, KV_WIDTH), blk3(1, KV_WIDTH), blk3(1, KV_WIDTH),
                  blk3(WINDOW, KV_WIDTH), blk3(WINDOW, KV_WIDTH)],
        out_specs=[blk3(N_HEADS, HEAD_DIM), blk3(WINDOW, KV_WIDTH), blk3(WINDOW, KV_WIDTH)],
        out_shape=[jax.ShapeDtypeStruct((nbatch, N_HEADS, HEAD_DIM), BF16),
                   jax.ShapeDtypeStruct((nbatch, WINDOW, KV_WIDTH), F32),
                   jax.ShapeDtypeStruct((nbatch, WINDOW, KV_WIDTH), F32)],
        compiler_params=_cparams("parallel"),
        name="attn_sample",
    )(sinks_col, slopes_col, q3, kn3, vn3, ck, cv)


CONV_HALO = 32
CONV_ROWS = 64


def _conv_prompt_kernel(ca0, ca1, cb0, cb1, w_ref, b_ref, g_ref, beta_ref,
                        c_ref, nc_ref, ext, shifted, ybuf, *, ts):
    i = pl.program_id(1)
    half = ca0.shape[1]
    span = shifted.shape[1]

    @pl.when(i == 0)
    def _():
        ext[0:CONV_HALO, :] = jnp.zeros((CONV_HALO, ext.shape[1]), F32)

    @pl.when(i > 0)
    def _():
        ext[0:CONV_HALO, :] = ext[ts:ts + CONV_HALO, :]

    ext[CONV_HALO:CONV_HALO + ts, 0:half] = ca0[...] * _sigmoid(cb0[...])
    ext[CONV_HALO:CONV_HALO + ts, half:2 * half] = ca1[...] * _sigmoid(cb1[...])

    for q in range(1, SUBLANES):
        shifted[q - 1] = ext[q:q + span, :]

    shift = CONV_HALO - (CONV_WIDTH - 1)
    for r in range(ts // CONV_ROWS):
        for c in range(ext.shape[1] // LANES):
            cols = slice(c * LANES, (c + 1) * LANES)
            acc = jnp.zeros((CONV_ROWS, LANES), F32)
            for j in range(CONV_WIDTH):
                q = (shift + j) % SUBLANES
                r0 = r * CONV_ROWS + shift + j - q
                rows = ext[r0:r0 + CONV_ROWS, cols] if q == 0 else shifted[q - 1, r0:r0 + CONV_ROWS, cols]
                acc = acc + w_ref[j:j + 1, cols] * rows
            ybuf[r * CONV_ROWS:(r + 1) * CONV_ROWS, cols] = acc

    y = ybuf[...] + b_ref[...]
    c_ref[...] = _silu(_layer_norm(y, g_ref[...], beta_ref[...])).astype(c_ref.dtype)

    @pl.when(i == pl.num_programs(1) - 1)
    def _():
        nc_ref[...] = ext[CONV_HALO + ts - (CONV_WIDTH - 1):CONV_HALO + ts, :]


def _conv_prompt(z2d, conv_w, conv_b, ln_g, ln_b, batch, seq, ts):
    nt = seq // ts
    ch = conv_w.shape[1]
    half = ch // 2
    ca_col = (ATTN_WIDTH + 2 * KV_WIDTH) // half
    cb_col = ca_col + 2
    zblk = lambda col: pl.BlockSpec((ts, half), lambda b, i: (b * nt + i, col))
    full = lambda a: pl.BlockSpec(a.shape, lambda b, i: (0,) * a.ndim)
    return pl.pallas_call(
        functools.partial(_conv_prompt_kernel, ts=ts),
        grid=(batch, nt),
        in_specs=[zblk(ca_col), zblk(ca_col + 1), zblk(cb_col), zblk(cb_col + 1),
                  full(conv_w), full(conv_b), full(ln_g), full(ln_b)],
        out_specs=[pl.BlockSpec((ts, ch), lambda b, i: (b * nt + i, 0)),
                   pl.BlockSpec((None, CONV_WIDTH - 1, ch), lambda b, i: (b, 0, 0))],
        out_shape=[jax.ShapeDtypeStruct((batch * seq, ch), BF16),
                   jax.ShapeDtypeStruct((batch, CONV_WIDTH - 1, ch), F32)],
        scratch_shapes=[pltpu.VMEM((CONV_HALO + ts, ch), F32),
                        pltpu.VMEM((SUBLANES - 1, CONV_HALO + ts - SUBLANES, ch), F32),
                        pltpu.VMEM((ts, ch), F32)],
        compiler_params=_cparams("parallel", "arbitrary"),
        name="conv_prompt",
    )(z2d, z2d, z2d, z2d, conv_w, conv_b, ln_g, ln_b)


def _conv_sample_kernel(st_ref, ca_ref, cb_ref, w_ref, b_ref, g_ref, beta_ref, c_ref, ns_ref, *, bb):
    u = ca_ref[...] * _sigmoid(cb_ref[...])
    w_past = w_ref[0:CONV_WIDTH - 1, :]
    w_new = w_ref[CONV_WIDTH - 1:CONV_WIDTH, :]
    for b in range(bb):
        st = st_ref[b]
        ub = u[b:b + 1, :]
        y = jnp.sum(st * w_past, axis=0, keepdims=True) + ub * w_new + b_ref[...]
        c_ref[b:b + 1, :] = _silu(_layer_norm(y, g_ref[...], beta_ref[...])).astype(c_ref.dtype)
        ns_ref[b, 0:CONV_WIDTH - 2, :] = st[1:CONV_WIDTH - 1, :]
        ns_ref[b, CONV_WIDTH - 2:CONV_WIDTH - 1, :] = ub


def _conv_sample(state, ca, cb, conv_w, conv_b, ln_g, ln_b, bb):
    nbatch, npast, ch = state.shape
    full = lambda a: pl.BlockSpec(a.shape, lambda i: (0,) * a.ndim)
    return pl.pallas_call(
        functools.partial(_conv_sample_kernel, bb=bb),
        grid=(nbatch // bb,),
        in_specs=[pl.BlockSpec((bb, npast, ch), lambda i: (i, 0, 0)),
                  pl.BlockSpec((bb, ch), lambda i: (i, 0)),
                  pl.BlockSpec((bb, ch), lambda i: (i, 0)),
                  full(conv_w), full(conv_b), full(ln_g), full(ln_b)],
        out_specs=[pl.BlockSpec((bb, ch), lambda i: (i, 0)),
                   pl.BlockSpec((bb, npast, ch), lambda i: (i, 0, 0))],
        out_shape=[jax.ShapeDtypeStruct((nbatch, ch), BF16),
                   jax.ShapeDtypeStruct((nbatch, npast, ch), F32)],
        compiler_params=_cparams("parallel"),
        name="conv_sample",
    )(state, ca, cb, conv_w, conv_b, ln_g, ln_b)


def _merge_kernel(*refs, n_gate_blocks):
    x_ref, o_ref, c_ref = refs[0:3]
    ga_refs = refs[3:3 + n_gate_blocks]
    gc_refs = refs[3 + n_gate_blocks:3 + 2 * n_gate_blocks]
    wap_ref, wcp_ref, wout_ref, g_ref, b_ref, out_ref = refs[3 + 2 * n_gate_blocks:]
    a = jnp.dot(o_ref[...], wap_ref[...], preferred_element_type=F32)
    cc = jnp.dot(c_ref[...], wcp_ref[...], preferred_element_type=F32)
    ga = jnp.concatenate([r[...] for r in ga_refs], axis=-1)
    gc = jnp.concatenate([r[...] for r in gc_refs], axis=-1)
    mixed = _sigmoid(ga) * a + _sigmoid(gc) * cc
    y = jnp.dot(mixed.astype(BF16), wout_ref[...], preferred_element_type=F32)
    out_ref[...] = _layer_norm(DEEPNORM_ALPHA * x_ref[...] + y, g_ref[...], b_ref[...])


def _merge(x2d, o2d, c2d, z2d, wap, wcp, wout, ln_g, ln_b, tm):
    m, d = x2d.shape
    gw = 512
    n_gate_blocks = d // gw
    ga_col = (z2d.shape[1] - 2 * d) // gw
    gc_col = ga_col + n_gate_blocks
    row = lambda w: pl.BlockSpec((tm, w), lambda i: (i, 0))
    zblk = lambda col: pl.BlockSpec((tm, gw), lambda i: (i, col))
    full = lambda a: pl.BlockSpec(a.shape, lambda i: (0,) * a.ndim)
    in_specs = ([row(d), row(o2d.shape[1]), row(c2d.shape[1])]
                + [zblk(ga_col + k) for k in range(n_gate_blocks)]
                + [zblk(gc_col + k) for k in range(n_gate_blocks)]
                + [full(wap), full(wcp), full(wout), full(ln_g), full(ln_b)])
    args = [x2d, o2d, c2d] + [z2d] * (2 * n_gate_blocks) + [wap, wcp, wout, ln_g, ln_b]
    return pl.pallas_call(
        functools.partial(_merge_kernel, n_gate_blocks=n_gate_blocks),
        grid=(m // tm,),
        in_specs=in_specs,
        out_specs=pl.BlockSpec((tm, d), lambda i: (i, 0)),
        out_shape=jax.ShapeDtypeStruct((m, d), F32),
        compiler_params=_cparams("parallel"),
        name="merge",
    )(*args)


def _router_kernel(xp_ref, xs_ref, wh_ref, wl_ref, b_ref, x_ref, idx_ref, gw_ref, rank_ref, cnt_ref, carry,
                   *, n_prompt_tiles):
    @pl.when(pl.program_id(0) == 0)
    def _():
        carry[...] = jnp.zeros(carry.shape, F32)

    x = jnp.where(pl.program_id(0) < n_prompt_tiles, xp_ref[...], xs_ref[...])
    x_ref[...] = x
    tr = x.shape[0]
    xh = x.astype(BF16)
    xl = (x - xh.astype(F32)).astype(BF16)
    wh = wh_ref[...]
    logits = (jnp.dot(xh, wh, preferred_element_type=F32)
              + jnp.dot(xl, wh, preferred_element_type=F32)
              + jnp.dot(xh, wl_ref[...], preferred_element_type=F32))
    scores = _sigmoid(logits)
    sel = scores + b_ref[...]
    lane = lax.broadcasted_iota(jnp.int32, (tr, N_EXPERTS), 1)
    lane_f = lane.astype(F32)
    grp = lane // GROUP_SIZE

    gscore = []
    for g in range(N_EXPERT_GROUPS):
        v = jnp.where(grp == g, sel, NEG_INF)
        m1 = jnp.max(v, axis=-1, keepdims=True)
        is_max = v == m1
        n_max = jnp.sum(is_max.astype(F32), axis=-1, keepdims=True)
        below = jnp.max(jnp.where(is_max, NEG_INF, v), axis=-1, keepdims=True)
        gscore.append(m1 + jnp.where(n_max > 1.5, m1, below))

    keep = jnp.zeros((tr, N_EXPERTS), jnp.bool_)
    for g in range(N_EXPERT_GROUPS):
        beaten = jnp.zeros((tr, 1), jnp.int32)
        for o in range(N_EXPERT_GROUPS):
            if o == g:
                continue
            wins = (gscore[o] > gscore[g]) | ((gscore[o] == gscore[g]) & (o < g))
            beaten = beaten + wins.astype(jnp.int32)
        keep = keep | ((beaten < TOPK_GROUPS) & (grp == g))

    cur = jnp.where(keep, sel, NEG_INF)
    picked = []
    weights = []
    onehot = jnp.zeros((tr, N_EXPERTS), F32)
    for _ in range(TOP_K):
        m = jnp.max(cur, axis=-1, keepdims=True)
        ik = jnp.min(jnp.where(cur == m, lane_f, float(N_EXPERTS)), axis=-1, keepdims=True)
        hit = lane_f == ik
        weights.append(jnp.sum(jnp.where(hit, scores, 0.0), axis=-1, keepdims=True))
        cur = jnp.where(hit, NEG_INF, cur)
        onehot = onehot + hit.astype(F32)
        picked.append(ik)
    wsum = weights[0]
    for w in weights[1:]:
        wsum = wsum + w

    ri = lax.broadcasted_iota(jnp.int32, (tr, tr), 0)
    ci = lax.broadcasted_iota(jnp.int32, (tr, tr), 1)
    tri = (ri > ci).astype(BF16)
    before = jnp.dot(tri, onehot.astype(BF16), preferred_element_type=F32) + carry[...]
    carry[...] = carry[...] + jnp.sum(onehot, axis=0, keepdims=True)
    cnt_ref[...] = carry[...]

    slot = lax.broadcasted_iota(jnp.int32, (tr, LANES), 1)
    idx_out = jnp.zeros((tr, LANES), jnp.int32)
    rank_out = jnp.zeros((tr, LANES), jnp.int32)
    gw_out = jnp.zeros((tr, LANES), F32)
    for k in range(TOP_K):
        rk = jnp.sum(jnp.where(lane_f == picked[k], before, 0.0), axis=-1, keepdims=True)
        idx_out = jnp.where(slot == k, picked[k].astype(jnp.int32), idx_out)
        rank_out = jnp.where(slot == k, rk.astype(jnp.int32), rank_out)
        gw_out = jnp.where(slot == k, weights[k] / wsum * ROUTED_SCALE, gw_out)
    idx_ref[...] = idx_out
    rank_ref[...] = rank_out
    gw_ref[...] = gw_out


def _router(x1_prompt, x1_sample, wr_hi, wr_lo, b_router):
    d = x1_prompt.shape[1]
    tr = TOKEN_TILE
    npt = x1_prompt.shape[0] // tr
    t = x1_prompt.shape[0] + x1_sample.shape[0]
    full = lambda a: pl.BlockSpec(a.shape, lambda i: (0,) * a.ndim)
    tile = pl.BlockSpec((tr, LANES), lambda i: (i, 0))
    return pl.pallas_call(
        functools.partial(_router_kernel, n_prompt_tiles=npt),
        grid=(t // tr,),
        in_specs=[pl.BlockSpec((tr, d), lambda i: (jnp.minimum(i, npt - 1), 0)),
                  pl.BlockSpec((tr, d), lambda i: (jnp.maximum(i - npt, 0), 0)),
                  full(wr_hi), full(wr_lo), full(b_router)],
        out_specs=[pl.BlockSpec((tr, d), lambda i: (i, 0)), tile, tile, tile,
                   pl.BlockSpec((1, N_EXPERTS), lambda i: (0, 0))],
        out_shape=[jax.ShapeDtypeStruct((t, d), F32),
                   jax.ShapeDtypeStruct((t, LANES), jnp.int32),
                   jax.ShapeDtypeStruct((t, LANES), F32),
                   jax.ShapeDtypeStruct((t, LANES), jnp.int32),
                   jax.ShapeDtypeStruct((1, N_EXPERTS), F32)],
        scratch_shapes=[pltpu.VMEM((1, N_EXPERTS), F32)],
        compiler_params=_cparams("arbitrary"),
        name="router",
    )(x1_prompt, x1_sample, wr_hi, wr_lo, b_router)


TOKEN_CHUNKS = 16
TOKEN_ROWS = TOKEN_CHUNKS + 2
PAD_CHUNKS = (64, 32, 16, 8, 4, 2, 1)


def _pack_rows(x, dst_ref, n_rows):
    for c in range(TOKEN_CHUNKS):
        dst_ref[pl.ds(c, n_rows, stride=TOKEN_ROWS), :] = x[:, c * LANES:(c + 1) * LANES]
    for c in range(TOKEN_CHUNKS, TOKEN_ROWS):
        dst_ref[pl.ds(c, n_rows, stride=TOKEN_ROWS), :] = jnp.zeros((n_rows, LANES), x.dtype)


def _unpack_rows(src_ref, slot0, n_rows):
    row0 = slot0 * TOKEN_ROWS
    return jnp.concatenate([src_ref[pl.ds(row0 + c, n_rows, stride=TOKEN_ROWS), :] for c in range(TOKEN_CHUNKS)],
                           axis=1)


def _slot_rows(slot, n_slots=1):
    return pl.ds(slot * TOKEN_ROWS, n_slots * TOKEN_ROWS)


TOKEN_WORD_ROWS = TOKEN_ROWS // 2


def _word_rows(slot, n_slots=1):
    return pl.ds(slot * TOKEN_WORD_ROWS, n_slots * TOKEN_WORD_ROWS)


def _dispatch_kernel(pstart_ref, cnt_ref, nact_ref, idx_ref, rank_ref, x_ref, xs_hbm, pbuf, zbuf, stage, sem, zsem,
                     *, n_blocks):
    i = pl.program_id(0)
    tt = TOKEN_TILE

    def chunk_fill(slot, p):
        return pltpu.make_async_copy(zbuf.at[_word_rows(0, p), :], xs_hbm.at[_word_rows(slot, p), :], zsem)

    def block_fill(b):
        return chunk_fill(b * MOE_BLOCK, MOE_BLOCK)

    def pad_fills(e, act):
        n = cnt_ref[e]
        slot = pstart_ref[e] + n
        padn = (MOE_BLOCK - (n & (MOE_BLOCK - 1))) & (MOE_BLOCK - 1)
        for p in PAD_CHUNKS:
            @pl.when((padn & p) != 0)
            def _():
                act(chunk_fill(slot, p))
            slot = slot + (padn & p)

    @pl.when(i == 0)
    def _():
        zbuf[...] = jnp.zeros(zbuf.shape, zbuf.dtype)

        def start_e(e, c):
            pad_fills(e, lambda cp: cp.start())
            return c

        def wait_e(e, c):
            pad_fills(e, lambda cp: cp.wait())
            return c

        def start_b(b, c):
            block_fill(b).start()
            return c

        def wait_b(b, c):
            block_fill(b).wait()
            return c

        lax.fori_loop(0, N_EXPERTS, start_e, 0)
        lax.fori_loop(nact_ref[0], n_blocks, start_b, 0)
        lax.fori_loop(0, N_EXPERTS, wait_e, 0)
        lax.fori_loop(nact_ref[0], n_blocks, wait_b, 0)

    _pack_rows(x_ref[...], stage, tt)
    pbuf[...] = pltpu.bitcast(stage[...].astype(BF16), jnp.uint32)

    def token(t, c):
        src = pbuf.at[_word_rows(t), :]
        for j in range(TOP_K):
            a = t * TOP_K + j
            d = pstart_ref[idx_ref[0, 0, a]] + rank_ref[0, 0, a]
            pltpu.make_async_copy(src, xs_hbm.at[_word_rows(d), :], sem).start(priority=j % 2)
        return c

    lax.fori_loop(0, tt, token, 0)
    for j in range(TOP_K):
        pltpu.make_async_copy(pbuf, xs_hbm.at[_word_rows(0, tt), :], sem).wait()


def _dispatch(x1, idx3, rank3, pstart, cnt, nact, n_blocks):
    t, d = x1.shape
    tt = TOKEN_TILE
    assert d == TOKEN_CHUNKS * LANES
    smem_tile = pl.BlockSpec((1, 1, tt * TOP_K), lambda i, *_: (i, 0, 0), memory_space=pltpu.SMEM)
    return pl.pallas_call(
        functools.partial(_dispatch_kernel, n_blocks=n_blocks),
        grid_spec=pltpu.PrefetchScalarGridSpec(
            num_scalar_prefetch=3,
            grid=(t // tt,),
            in_specs=[smem_tile, smem_tile, pl.BlockSpec((tt, d), lambda i, *_: (i, 0))],
            out_specs=pl.BlockSpec(memory_space=pl.ANY),
            scratch_shapes=[pltpu.VMEM((tt * TOKEN_WORD_ROWS, LANES), jnp.uint32),
                            pltpu.VMEM((MOE_BLOCK * TOKEN_WORD_ROWS, LANES), jnp.uint32),
                            pltpu.VMEM((tt * TOKEN_ROWS, LANES), F32),
                            pltpu.SemaphoreType.DMA, pltpu.SemaphoreType.DMA]),
        out_shape=jax.ShapeDtypeStruct((n_blocks * MOE_BLOCK * TOKEN_WORD_ROWS, LANES), jnp.uint32),
        compiler_params=pltpu.CompilerParams(dimension_semantics=("arbitrary",),
                                             vmem_limit_bytes=VMEM_LIMIT, has_side_effects=True),
        name="dispatch",
    )(pstart, cnt, nact, idx3, rank3, x1)


WEIGHT_DMA_PRIORITY = 1


def _experts_kernel(bexp_ref, nact_ref, xs_ref, wg_hbm, wu_hbm, wd_hbm, y_ref,
                    wg_f32, wu_f32, wd_f32, wg_bf, wu_bf, wd_bf, stage, slot_ref, sems):
    b = pl.program_id(0)
    nact = nact_ref[0]
    last = pl.num_programs(0) - 1
    active = b < nact
    e = bexp_ref[b]
    new_expert = (b == 0) | (e != bexp_ref[jnp.maximum(b - 1, 0)])

    def fetch(expert, slot):
        return (pltpu.make_async_copy(wg_hbm.at[expert], wg_f32.at[slot], sems.at[slot, 0]),
                pltpu.make_async_copy(wu_hbm.at[expert], wu_f32.at[slot], sems.at[slot, 1]),
                pltpu.make_async_copy(wd_hbm.at[expert], wd_f32.at[slot], sems.at[slot, 2]))

    @pl.when(active & new_expert)
    def _():
        @pl.when(b == 0)
        def _():
            slot_ref[0] = 0
            for cp in fetch(e, 0):
                cp.start(priority=WEIGHT_DMA_PRIORITY)

        @pl.when(b > 0)
        def _():
            slot_ref[0] = 1 - slot_ref[0]

        slot = slot_ref[0]
        nxt = lax.while_loop(lambda j: (j < nact) & (bexp_ref[jnp.minimum(j, last)] == e),
                             lambda j: j + 1, b + 1)

        @pl.when(nxt < nact)
        def _():
            for cp in fetch(bexp_ref[jnp.minimum(nxt, last)], 1 - slot):
                cp.start(priority=WEIGHT_DMA_PRIORITY)

        cp_g, cp_u, cp_d = fetch(e, slot)
        cp_g.wait()
        wg_bf[...] = wg_f32[slot].astype(BF16)
        cp_u.wait()
        wu_bf[...] = wu_f32[slot].astype(BF16)
        cp_d.wait()
        wd_bf[...] = wd_f32[slot].astype(BF16)

    @pl.when(active)
    def _():
        stage[...] = pltpu.bitcast(xs_ref[...], BF16).astype(F32)
        x = _unpack_rows(stage, 0, MOE_BLOCK).astype(BF16)
        g = jnp.dot(x, wg_bf[...], preferred_element_type=F32)
        u = jnp.dot(x, wu_bf[...], preferred_element_type=F32)
        h = (_silu(g) * u).astype(BF16)
        _pack_rows(jnp.dot(h, wd_bf[...], preferred_element_type=F32), y_ref, MOE_BLOCK)

    @pl.when(jnp.logical_not(active))
    def _():
        y_ref[...] = jnp.zeros(y_ref.shape, y_ref.dtype)


def _experts(xs, w_gate_e, w_up_e, w_down_e, block_exp, nact):
    n_blocks = xs.shape[0] // (MOE_BLOCK * TOKEN_WORD_ROWS)
    _, d, ff = w_gate_e.shape
    live = lambda b, bexp, nact: (jnp.minimum(b, nact[0] - 1), 0)
    hbm = pl.BlockSpec(memory_space=pl.ANY)
    return pl.pallas_call(
        _experts_kernel,
        grid_spec=pltpu.PrefetchScalarGridSpec(
            num_scalar_prefetch=2,
            grid=(n_blocks,),
            in_specs=[pl.BlockSpec((MOE_BLOCK * TOKEN_WORD_ROWS, LANES), live), hbm, hbm, hbm],
            out_specs=pl.BlockSpec((MOE_BLOCK * TOKEN_ROWS, LANES), lambda b, bexp, nact: (b, 0)),
            scratch_shapes=[pltpu.VMEM((2, d, ff), F32), pltpu.VMEM((2, d, ff), F32), pltpu.VMEM((2, ff, d), F32),
                            pltpu.VMEM((d, ff), BF16), pltpu.VMEM((d, ff), BF16), pltpu.VMEM((ff, d), BF16),
                            pltpu.VMEM((MOE_BLOCK * TOKEN_ROWS, LANES), F32),
                            pltpu.SMEM((1,), jnp.int32), pltpu.SemaphoreType.DMA((2, 3))]),
        out_shape=jax.ShapeDtypeStruct((n_blocks * MOE_BLOCK * TOKEN_ROWS, LANES), F32),
        compiler_params=_cparams("arbitrary"),
        name="experts",
    )(block_exp, nact, xs, w_gate_e, w_up_e, w_down_e)


def _combine_kernel(pstart_ref, idx_ref, rank_ref, x_ref, gw_ref, y_hbm, wgs_ref, wus_ref, wds_ref, g_ref, b_ref,
                    op_ref, os_ref, ybuf, sem, *, n_prompt_tiles):
    i = pl.program_id(0)
    tt = TOKEN_TILE

    def token(t, c):
        for j in range(TOP_K):
            a = t * TOP_K + j
            d = pstart_ref[idx_ref[0, 0, a]] + rank_ref[0, 0, a]
            pltpu.make_async_copy(y_hbm.at[_slot_rows(d), :], ybuf.at[_slot_rows(j * tt + t), :],
                                  sem).start(priority=j % 2)
        return c

    lax.fori_loop(0, tt, token, 0)

    x = x_ref[...]
    xb = x.astype(BF16)
    hs = _silu(jnp.dot(xb, wgs_ref[...], preferred_element_type=F32)) * jnp.dot(
        xb, wus_ref[...], preferred_element_type=F32)
    f = jnp.dot(hs.astype(BF16), wds_ref[...], preferred_element_type=F32)

    for j in range(TOP_K):
        pltpu.make_async_copy(y_hbm.at[_slot_rows(0, tt), :], ybuf.at[_slot_rows(j * tt, tt), :], sem).wait()
    gw = gw_ref[...]
    for j in range(TOP_K):
        f = f + gw[:, j:j + 1] * _unpack_rows(ybuf, j * tt, tt)
    out = _layer_norm(DEEPNORM_ALPHA * x + f, g_ref[...], b_ref[...])

    @pl.when(i < n_prompt_tiles)
    def _():
        op_ref[...] = out

    @pl.when(i >= n_prompt_tiles)
    def _():
        os_ref[...] = out


def _combine(x1, idx3, rank3, pstart, gw, y, wgs, wus, wds, ln_g, ln_b, n_prompt_rows):
    t, d = x1.shape
    tt = TOKEN_TILE
    nt = t // tt
    npt = n_prompt_rows // tt
    full = lambda a: pl.BlockSpec(a.shape, lambda i, *_: (0,) * a.ndim)
    smem_tile = pl.BlockSpec((1, 1, tt * TOP_K), lambda i, *_: (i, 0, 0), memory_space=pltpu.SMEM)
    return pl.pallas_call(
        functools.partial(_combine_kernel, n_prompt_tiles=npt),
        grid_spec=pltpu.PrefetchScalarGridSpec(
            num_scalar_prefetch=1,
            grid=(nt,),
            in_specs=[smem_tile, smem_tile,
                      pl.BlockSpec((tt, d), lambda i, *_: (i, 0)),
                      pl.BlockSpec((tt, LANES), lambda i, *_: (i, 0)),
                      pl.BlockSpec(memory_space=pl.ANY),
                      full(wgs), full(wus), full(wds), full(ln_g), full(ln_b)],
            out_specs=[pl.BlockSpec((tt, d), lambda i, *_: (jnp.minimum(i, npt - 1), 0)),
                       pl.BlockSpec((tt, d), lambda i, *_: (jnp.maximum(i - npt, 0), 0))],
            scratch_shapes=[pltpu.VMEM((TOP_K * tt * TOKEN_ROWS, LANES), F32), pltpu.SemaphoreType.DMA]),
        out_shape=[jax.ShapeDtypeStruct((n_prompt_rows, d), F32),
                   jax.ShapeDtypeStruct((t - n_prompt_rows, d), F32)],
        compiler_params=_cparams("arbitrary"),
        name="combine",
    )(pstart, idx3, rank3, x1, gw, y, wgs, wus, wds, ln_g, ln_b)


def _moe_plan(idx, rank, counts, n_tokens):
    cnt = counts[0].astype(jnp.int32)
    padded = ((cnt + MOE_BLOCK - 1) // MOE_BLOCK) * MOE_BLOCK
    pend = jnp.cumsum(padded)
    pstart = pend - padded
    n_assign = n_tokens * TOP_K
    n_blocks = (n_assign + N_EXPERTS * (MOE_BLOCK - 1)) // MOE_BLOCK
    tile3 = lambda a: a[:, :TOP_K].reshape(n_tokens // TOKEN_TILE, 1, TOKEN_TILE * TOP_K)
    nact = (pend[-1:] // MOE_BLOCK).astype(jnp.int32)
    block_start = jnp.arange(n_blocks, dtype=jnp.int32) * MOE_BLOCK
    block_exp = jnp.minimum(jnp.sum(pend[None, :] <= block_start[:, None], axis=1), N_EXPERTS - 1)
    return tile3(idx), tile3(rank), pstart.astype(jnp.int32), cnt, nact, block_exp.astype(jnp.int32), n_blocks


def kernel(x_prompt, x_sample, cache_k, cache_v, state_conv, w_in, sinks, w_attn_proj, conv_w, conv_b,
           conv_ln_g, conv_ln_b, w_conv_proj, w_out, ln1_g, ln1_b, w_router, b_router, w_gate_e, w_up_e,
           w_down_e, w_gate_s, w_up_s, w_down_s, ln2_g, ln2_b):
    assert w_in.shape[0] == DEPTH
    batch, seq, d = x_prompt.shape
    nbatch = x_sample.shape[0]
    n_prompt = batch * seq
    n_tokens = n_prompt + nbatch
    ch = conv_w.shape[2]
    row = lambda a: a[0].reshape(1, -1)

    w_in_bf = w_in[0].astype(BF16)
    wap = w_attn_proj[0].astype(BF16)
    wcp = w_conv_proj[0].astype(BF16)
    wout = w_out[0].astype(BF16)
    wgs = w_gate_s[0].astype(BF16)
    wus = w_up_s[0].astype(BF16)
    wds = w_down_s[0].astype(BF16)
    wr = w_router[0].astype(F32)
    wr_hi = wr.astype(BF16)
    wr_lo = (wr - wr_hi.astype(F32)).astype(BF16)

    xp2d = x_prompt.reshape(n_prompt, d)
    xs2d = x_sample.reshape(nbatch, d)

    zp = _inproj(xp2d, w_in_bf, tm=1024, tn=1536)
    o_p = _attn_prompt(zp, sinks[0], batch, seq)
    c_p, newconv_p = _conv_prompt(zp, conv_w[0], row(conv_b), row(conv_ln_g), row(conv_ln_b), batch, seq, ts=256)
    x1_p = _merge(xp2d, o_p, c_p, zp, wap, wcp, wout, row(ln1_g), row(ln1_b), tm=256)
    k_off, v_off, ca_off = ATTN_WIDTH, ATTN_WIDTH + KV_WIDTH, ATTN_WIDTH + 2 * KV_WIDTH
    zp3 = zp.reshape(batch, seq, -1)
    newk_p = zp3[:, seq - WINDOW:, k_off:v_off].reshape(1, batch, WINDOW, N_KV_HEADS, HEAD_DIM)
    newv_p = zp3[:, seq - WINDOW:, v_off:ca_off].reshape(1, batch, WINDOW, N_KV_HEADS, HEAD_DIM)

    zs = _inproj(xs2d, w_in_bf, tm=nbatch, tn=1536)
    own_block = (jnp.arange(KV_WIDTH)[None, :] // HEAD_DIM) == (jnp.arange(N_HEADS)[:, None] // GQA_GROUP)
    q3 = jnp.where(own_block[None], jnp.tile(zs[:, :k_off].reshape(nbatch, N_HEADS, HEAD_DIM), (1, 1, N_KV_HEADS)), 0.0)
    kn3 = zs[:, k_off:v_off].reshape(nbatch, 1, KV_WIDTH)
    vn3 = zs[:, v_off:ca_off].reshape(nbatch, 1, KV_WIDTH)
    slopes_col = jnp.asarray(ALIBI_SLOPES, F32).reshape(N_HEADS, 1)
    o_s3, newk_s, newv_s = _attn_sample(
        q3, kn3, vn3, cache_k[0].reshape(nbatch, WINDOW, KV_WIDTH), cache_v[0].reshape(nbatch, WINDOW, KV_WIDTH),
        sinks[0].reshape(N_HEADS, 1), slopes_col, bb=8)
    c_s, newconv_s = _conv_sample(state_conv[0], zs[:, ca_off:ca_off + ch], zs[:, ca_off + ch:ca_off + 2 * ch],
                                  conv_w[0], row(conv_b), row(conv_ln_g), row(conv_ln_b), bb=8)
    x1_s = _merge(xs2d, o_s3.reshape(nbatch, ATTN_WIDTH), c_s, zs, wap, wcp, wout, row(ln1_g), row(ln1_b),
                  tm=nbatch)

    x1, idx, gw, rank, counts = _router(x1_p, x1_s, wr_hi, wr_lo, row(b_router))
    idx3, rank3, pstart, cnt, nact, block_exp, n_blocks = _moe_plan(idx, rank, counts, n_tokens)
    xs_sorted = _dispatch(x1, idx3, rank3, pstart, cnt, nact, n_blocks)
    y_sorted = _experts(xs_sorted, w_gate_e[0], w_up_e[0], w_down_e[0], block_exp, nact)
    y_p, y_s = _combine(x1, idx3, rank3, pstart, gw, y_sorted, wgs, wus, wds, row(ln2_g), row(ln2_b), n_prompt)

    return (y_p.reshape(batch, seq, d), y_s.reshape(nbatch, 1, d),
            newk_p, newv_p, newconv_p.reshape(1, batch, CONV_WIDTH - 1, ch),
            newk_s.reshape(1, nbatch, WINDOW, N_KV_HEADS, HEAD_DIM),
            newv_s.reshape(1, nbatch, WINDOW, N_KV_HEADS, HEAD_DIM),
            newconv_s.reshape(1, nbatch, CONV_WIDTH - 1, ch))
```

```python
import functools

import jax
import jax.numpy as jnp
from jax import lax
from jax.experimental import pallas as pl
from jax.experimental.pallas import tpu as pltpu

HEAD_DIM = 64
N_HEADS = 16
N_KV_HEADS = 4
GQA_GROUP = N_HEADS // N_KV_HEADS
WINDOW = 128
ATTN_WIDTH = N_HEADS * HEAD_DIM
KV_WIDTH = N_KV_HEADS * HEAD_DIM
CONV_WIDTH = 31
N_EXPERTS = 256
TOP_K = 8
N_EXPERT_GROUPS = 8
GROUP_SIZE = N_EXPERTS // N_EXPERT_GROUPS
TOPK_GROUPS = 4
ROUTED_SCALE = 2.5
LN_EPS = 1e-5
DEPTH = 1
DEEPNORM_ALPHA = (2 * DEPTH) ** 0.25
ALIBI_SLOPES = tuple(2.0 ** (-8.0 * (h + 1.0) / N_HEADS) for h in range(N_HEADS))
ATTN_SCALE = HEAD_DIM ** -0.5

MOE_BLOCK = 128
TOKEN_TILE = 128
LANES = 128
SUBLANES = 8
VMEM_LIMIT = 56 * 1024 * 1024

BF16 = jnp.bfloat16
F32 = jnp.float32
NEG_INF = float("-inf")


def _cparams(*sem):
    return pltpu.CompilerParams(dimension_semantics=sem, vmem_limit_bytes=VMEM_LIMIT)


def _sigmoid(x):
    return 1.0 / (1.0 + jnp.exp(-x))


def _silu(x):
    return x * _sigmoid(x)


def _layer_norm(x, g, b):
    mu = jnp.mean(x, axis=-1, keepdims=True)
    xc = x - mu
    var = jnp.mean(xc * xc, axis=-1, keepdims=True)
    return xc * lax.rsqrt(var + LN_EPS) * g + b


def _inproj_kernel(x_ref, w_ref, o_ref, xb_ref):
    @pl.when(pl.program_id(1) == 0)
    def _():
        xb_ref[...] = x_ref[...].astype(BF16)

    o_ref[...] = jnp.dot(xb_ref[...], w_ref[...], preferred_element_type=F32)


def _inproj(x2d, w_bf, tm, tn):
    m, k = x2d.shape
    n = w_bf.shape[1]
    return pl.pallas_call(
        _inproj_kernel,
        grid=(m // tm, n // tn),
        in_specs=[pl.BlockSpec((tm, k), lambda i, j: (i, 0)),
                  pl.BlockSpec((k, tn), lambda i, j: (0, j))],
        out_specs=pl.BlockSpec((tm, tn), lambda i, j: (i, j)),
        out_shape=jax.ShapeDtypeStruct((m, n), F32),
        scratch_shapes=[pltpu.VMEM((tm, k), BF16)],
        compiler_params=_cparams("parallel", "arbitrary"),
        name="inproj",
    )(x2d, w_bf)


def _attn_prompt_kernel(sinks_ref, q_ref, kp_ref, kc_ref, vp_ref, vc_ref, o_ref):
    i = pl.program_id(1)
    q = q_ref[...]
    k = jnp.concatenate([kp_ref[...], kc_ref[...]], axis=0).astype(BF16)
    v = jnp.concatenate([vp_ref[...], vc_ref[...]], axis=0).astype(BF16)
    qi = lax.broadcasted_iota(jnp.int32, (WINDOW, 2 * WINDOW), 0)
    kj = lax.broadcasted_iota(jnp.int32, (WINDOW, 2 * WINDOW), 1)
    dist = qi + WINDOW - kj
    valid = (dist >= 0) & (dist <= WINDOW) & ((kj >= WINDOW) | (i > 0))
    distf = dist.astype(F32)
    outs = []
    for h in range(N_HEADS):
        kv = h // GQA_GROUP
        qh = q[:, h * HEAD_DIM:(h + 1) * HEAD_DIM].astype(BF16)
        kh = k[:, kv * HEAD_DIM:(kv + 1) * HEAD_DIM]
        vh = v[:, kv * HEAD_DIM:(kv + 1) * HEAD_DIM]
        s = lax.dot_general(qh, kh, (((1,), (1,)), ((), ())), preferred_element_type=F32) * ATTN_SCALE
        s = jnp.where(valid, s - ALIBI_SLOPES[h] * distf, NEG_INF)
        sink = sinks_ref[h]
        m = jnp.maximum(jnp.max(s, axis=-1, keepdims=True), sink)
        e = jnp.exp(s - m)
        denom = jnp.sum(e, axis=-1, keepdims=True) + jnp.exp(sink - m)
        p = (e / denom).astype(BF16)
        outs.append(jnp.dot(p, vh, preferred_element_type=F32))
    o_ref[...] = jnp.concatenate(outs, axis=-1).astype(o_ref.dtype)


def _attn_prompt(z2d, sinks, batch, seq):
    nb = seq // WINDOW
    kcol = ATTN_WIDTH // KV_WIDTH
    vcol = kcol + 1

    def cur(col):
        return lambda b, i: (b * nb + i, col)

    def prev(col):
        return lambda b, i: (b * nb + jnp.maximum(i - 1, 0), col)

    return pl.pallas_call(
        _attn_prompt_kernel,
        grid=(batch, nb),
        in_specs=[pl.BlockSpec(memory_space=pltpu.SMEM),
                  pl.BlockSpec((WINDOW, ATTN_WIDTH), cur(0)),
                  pl.BlockSpec((WINDOW, KV_WIDTH), prev(kcol)),
                  pl.BlockSpec((WINDOW, KV_WIDTH), cur(kcol)),
                  pl.BlockSpec((WINDOW, KV_WIDTH), prev(vcol)),
                  pl.BlockSpec((WINDOW, KV_WIDTH), cur(vcol))],
        out_specs=pl.BlockSpec((WINDOW, ATTN_WIDTH), cur(0)),
        out_shape=jax.ShapeDtypeStruct((batch * seq, ATTN_WIDTH), BF16),
        compiler_params=_cparams("parallel", "parallel"),
        name="attn_prompt",
    )(sinks, z2d, z2d, z2d, z2d, z2d)


def _attn_sample_kernel(sinks_ref, slopes_ref, q_ref, kn_ref, vn_ref, ck_ref, cv_ref,
                        o_ref, nk_ref, nv_ref, *, bb):
    dist_c = (WINDOW - lax.broadcasted_iota(jnp.int32, (1, WINDOW), 1)).astype(F32)
    sink = sinks_ref[...]
    bias_c = slopes_ref[...] * dist_c
    head = lax.broadcasted_iota(jnp.int32, (N_HEADS, KV_WIDTH), 0)
    lane = lax.broadcasted_iota(jnp.int32, (N_HEADS, KV_WIDTH), 1)
    own = (lane // HEAD_DIM) == (head // GQA_GROUP)
    for b in range(bb):
        qb = q_ref[b].astype(BF16)
        kc = ck_ref[b]
        vc = cv_ref[b]
        kn = kn_ref[b]
        vn = vn_ref[b]
        sc = lax.dot_general(qb, kc.astype(BF16), (((1,), (1,)), ((), ())),
                             preferred_element_type=F32) * ATTN_SCALE - bias_c
        sn = jnp.sum(qb.astype(F32) * kn.astype(BF16).astype(F32), axis=-1, keepdims=True) * ATTN_SCALE
        m = jnp.maximum(jnp.maximum(jnp.max(sc, axis=-1, keepdims=True), sn), sink)
        ec = jnp.exp(sc - m)
        en = jnp.exp(sn - m)
        denom = jnp.sum(ec, axis=-1, keepdims=True) + en + jnp.exp(sink - m)
        pc = (ec / denom).astype(BF16)
        pn = (en / denom).astype(BF16).astype(F32)
        r = jnp.dot(pc, vc.astype(BF16), preferred_element_type=F32) + pn * vn.astype(BF16).astype(F32)
        r = jnp.where(own, r, 0.0)
        o = r[:, 0:HEAD_DIM]
        for kv in range(1, N_KV_HEADS):
            o = o + r[:, kv * HEAD_DIM:(kv + 1) * HEAD_DIM]
        o_ref[b] = o.astype(o_ref.dtype)
        nk_ref[b, 0:WINDOW - 1, :] = kc[1:WINDOW, :]
        nk_ref[b, WINDOW - 1:WINDOW, :] = kn
        nv_ref[b, 0:WINDOW - 1, :] = vc[1:WINDOW, :]
        nv_ref[b, WINDOW - 1:WINDOW, :] = vn


def _attn_sample(q3, kn3, vn3, ck, cv, sinks_col, slopes_col, bb):
    nbatch = q3.shape[0]
    blk3 = lambda s1, s2: pl.BlockSpec((bb, s1, s2), lambda i: (i, 0, 0))
    full = lambda a: pl.BlockSpec(a.shape, lambda i: (0,) * a.ndim)
    return pl.pallas_call(
        functools.partial(_attn_sample_kernel, bb=bb),
        grid=(nbatch // bb,),
        in_specs=[full(sinks_col), full(slopes_col),
                  blk3(N_HEADS, KV_WIDTH), blk3(1, KV_WIDTH), blk3(1, KV_WIDTH),
                  blk3(WINDOW, KV_WIDTH), blk3(WINDOW, KV_WIDTH)],
        out_specs=[blk3(N_HEADS, HEAD_DIM), blk3(WINDOW, KV_WIDTH), blk3(WINDOW, KV_WIDTH)],
        out_shape=[jax.ShapeDtypeStruct((nbatch, N_HEADS, HEAD_DIM), BF16),
                   jax.ShapeDtypeStruct((nbatch, WINDOW, KV_WIDTH), F32),
                   jax.ShapeDtypeStruct((nbatch, WINDOW, KV_WIDTH), F32)],
        compiler_params=_cparams("parallel"),
        name="attn_sample",
    )(sinks_col, slopes_col, q3, kn3, vn3, ck, cv)


CONV_HALO = 32
CONV_ROWS = 64


def _conv_prompt_kernel(ca0, ca1, cb0, cb1, w_ref, b_ref, g_ref, beta_ref,
                        c_ref, nc_ref, ext, shifted, ybuf, *, ts):
    i = pl.program_id(1)
    half = ca0.shape[1]
    span = shifted.shape[1]

    @pl.when(i == 0)
    def _():
        ext[0:CONV_HALO, :] = jnp.zeros((CONV_HALO, ext.shape[1]), F32)

    @pl.when(i > 0)
    def _():
        ext[0:CONV_HALO, :] = ext[ts:ts + CONV_HALO, :]

    ext[CONV_HALO:CONV_HALO + ts, 0:half] = ca0[...] * _sigmoid(cb0[...])
    ext[CONV_HALO:CONV_HALO + ts, half:2 * half] = ca1[...] * _sigmoid(cb1[...])

    for q in range(1, SUBLANES):
        shifted[q - 1] = ext[q:q + span, :]

    shift = CONV_HALO - (CONV_WIDTH - 1)
    for r in range(ts // CONV_ROWS):
        for c in range(ext.shape[1] // LANES):
            cols = slice(c * LANES, (c + 1) * LANES)
            acc = jnp.zeros((CONV_ROWS, LANES), F32)
            for j in range(CONV_WIDTH):
                q = (shift + j) % SUBLANES
                r0 = r * CONV_ROWS + shift + j - q
                rows = ext[r0:r0 + CONV_ROWS, cols] if q == 0 else shifted[q - 1, r0:r0 + CONV_ROWS, cols]
                acc = acc + w_ref[j:j + 1, cols] * rows
            ybuf[r * CONV_ROWS:(r + 1) * CONV_ROWS, cols] = acc

    y = ybuf[...] + b_ref[...]
    c_ref[...] = _silu(_layer_norm(y, g_ref[...], beta_ref[...])).astype(c_ref.dtype)

    @pl.when(i == pl.num_programs(1) - 1)
    def _():
        nc_ref[...] = ext[CONV_HALO + ts - (CONV_WIDTH - 1):CONV_HALO + ts, :]


def _conv_prompt(z2d, conv_w, conv_b, ln_g, ln_b, batch, seq, ts):
    nt = seq // ts
    ch = conv_w.shape[1]
    half = ch // 2
    ca_col = (ATTN_WIDTH + 2 * KV_WIDTH) // half
    cb_col = ca_col + 2
    zblk = lambda col: pl.BlockSpec((ts, half), lambda b, i: (b * nt + i, col))
    full = lambda a: pl.BlockSpec(a.shape, lambda b, i: (0,) * a.ndim)
    return pl.pallas_call(
        functools.partial(_conv_prompt_kernel, ts=ts),
        grid=(batch, nt),
        in_specs=[zblk(ca_col), zblk(ca_col + 1), zblk(cb_col), zblk(cb_col + 1),
                  full(conv_w), full(conv_b), full(ln_g), full(ln_b)],
        out_specs=[pl.BlockSpec((ts, ch), lambda b, i: (b * nt + i, 0)),
                   pl.BlockSpec((None, CONV_WIDTH - 1, ch), lambda b, i: (b, 0, 0))],
        out_shape=[jax.ShapeDtypeStruct((batch * seq, ch), BF16),
                   jax.ShapeDtypeStruct((batch, CONV_WIDTH - 1, ch), F32)],
        scratch_shapes=[pltpu.VMEM((CONV_HALO + ts, ch), F32),
                        pltpu.VMEM((SUBLANES - 1, CONV_HALO + ts - SUBLANES, ch), F32),
                        pltpu.VMEM((ts, ch), F32)],
        compiler_params=_cparams("parallel", "arbitrary"),
        name="conv_prompt",
    )(z2d, z2d, z2d, z2d, conv_w, conv_b, ln_g, ln_b)


def _conv_sample_kernel(st_ref, ca_ref, cb_ref, w_ref, b_ref, g_ref, beta_ref, c_ref, ns_ref, *, bb):
    u = ca_ref[...] * _sigmoid(cb_ref[...])
    w_past = w_ref[0:CONV_WIDTH - 1, :]
    w_new = w_ref[CONV_WIDTH - 1:CONV_WIDTH, :]
    for b in range(bb):
        st = st_ref[b]
        ub = u[b:b + 1, :]
        y = jnp.sum(st * w_past, axis=0, keepdims=True) + ub * w_new + b_ref[...]
        c_ref[b:b + 1, :] = _silu(_layer_norm(y, g_ref[...], beta_ref[...])).astype(c_ref.dtype)
        ns_ref[b, 0:CONV_WIDTH - 2, :] = st[1:CONV_WIDTH - 1, :]
        ns_ref[b, CONV_WIDTH - 2:CONV_WIDTH - 1, :] = ub


def _conv_sample(state, ca, cb, conv_w, conv_b, ln_g, ln_b, bb):
    nbatch, npast, ch = state.shape
    full = lambda a: pl.BlockSpec(a.shape, lambda i: (0,) * a.ndim)
    return pl.pallas_call(
        functools.partial(_conv_sample_kernel, bb=bb),
        grid=(nbatch // bb,),
        in_specs=[pl.BlockSpec((bb, npast, ch), lambda i: (i, 0, 0)),
                  pl.BlockSpec((bb, ch), lambda i: (i, 0)),
                  pl.BlockSpec((bb, ch), lambda i: (i, 0)),
                  full(conv_w), full(conv_b), full(ln_g), full(ln_b)],
        out_specs=[pl.BlockSpec((bb, ch), lambda i: (i, 0)),
                   pl.BlockSpec((bb, npast, ch), lambda i: (i, 0, 0))],
        out_shape=[jax.ShapeDtypeStruct((nbatch, ch), BF16),
                   jax.ShapeDtypeStruct((nbatch, npast, ch), F32)],
        compiler_params=_cparams("parallel"),
        name="conv_sample",
    )(state, ca, cb, conv_w, conv_b, ln_g, ln_b)


def _merge_kernel(*refs, n_gate_blocks):
    x_ref, o_ref, c_ref = refs[0:3]
    ga_refs = refs[3:3 + n_gate_blocks]
    gc_refs = refs[3 + n_gate_blocks:3 + 2 * n_gate_blocks]
    wap_ref, wcp_ref, wout_ref, g_ref, b_ref, out_ref = refs[3 + 2 * n_gate_blocks:]
    a = jnp.dot(o_ref[...], wap_ref[...], preferred_element_type=F32)
    cc = jnp.dot(c_ref[...], wcp_ref[...], preferred_element_type=F32)
    ga = jnp.concatenate([r[...] for r in ga_refs], axis=-1)
    gc = jnp.concatenate([r[...] for r in gc_refs], axis=-1)
    mixed = _sigmoid(ga) * a + _sigmoid(gc) * cc
    y = jnp.dot(mixed.astype(BF16), wout_ref[...], preferred_element_type=F32)
    out_ref[...] = _layer_norm(DEEPNORM_ALPHA * x_ref[...] + y, g_ref[...], b_ref[...])


def _merge(x2d, o2d, c2d, z2d, wap, wcp, wout, ln_g, ln_b, tm):
    m, d = x2d.shape
    gw = 512
    n_gate_blocks = d // gw
    ga_col = (z2d.shape[1] - 2 * d) // gw
    gc_col = ga_col + n_gate_blocks
    row = lambda w: pl.BlockSpec((tm, w), lambda i: (i, 0))
    zblk = lambda col: pl.BlockSpec((tm, gw), lambda i: (i, col))
    full = lambda a: pl.BlockSpec(a.shape, lambda i: (0,) * a.ndim)
    in_specs = ([row(d), row(o2d.shape[1]), row(c2d.shape[1])]
                + [zblk(ga_col + k) for k in range(n_gate_blocks)]
                + [zblk(gc_col + k) for k in range(n_gate_blocks)]
                + [full(wap), full(wcp), full(wout), full(ln_g), full(ln_b)])
    args = [x2d, o2d, c2d] + [z2d] * (2 * n_gate_blocks) + [wap, wcp, wout, ln_g, ln_b]
    return pl.pallas_call(
        functools.partial(_merge_kernel, n_gate_blocks=n_gate_blocks),
        grid=(m // tm,),
        in_specs=in_specs,
        out_specs=pl.BlockSpec((tm, d), lambda i: (i, 0)),
        out_shape=jax.ShapeDtypeStruct((m, d), F32),
        compiler_params=_cparams("parallel"),
        name="merge",
    )(*args)


def _router_kernel(xp_ref, xs_ref, wh_ref, wl_ref, b_ref, x_ref, idx_ref, gw_ref, rank_ref, cnt_ref, carry,
                   *, n_prompt_tiles):
    @pl.when(pl.program_id(0) == 0)
    def _():
        carry[...] = jnp.zeros(carry.shape, F32)

    x = jnp.where(pl.program_id(0) < n_prompt_tiles, xp_ref[...], xs_ref[...])
    x_ref[...] = x
    tr = x.shape[0]
    xh = x.astype(BF16)
    xl = (x - xh.astype(F32)).astype(BF16)
    wh = wh_ref[...]
    logits = (jnp.dot(xh, wh, preferred_element_type=F32)
              + jnp.dot(xl, wh, preferred_element_type=F32)
              + jnp.dot(xh, wl_ref[...], preferred_element_type=F32))
    scores = _sigmoid(logits)
    sel = scores + b_ref[...]
    lane = lax.broadcasted_iota(jnp.int32, (tr, N_EXPERTS), 1)
    lane_f = lane.astype(F32)
    grp = lane // GROUP_SIZE

    gscore = []
    for g in range(N_EXPERT_GROUPS):
        v = jnp.where(grp == g, sel, NEG_INF)
        m1 = jnp.max(v, axis=-1, keepdims=True)
        is_max = v == m1
        n_max = jnp.sum(is_max.astype(F32), axis=-1, keepdims=True)
        below = jnp.max(jnp.where(is_max, NEG_INF, v), axis=-1, keepdims=True)
        gscore.append(m1 + jnp.where(n_max > 1.5, m1, below))

    keep = jnp.zeros((tr, N_EXPERTS), jnp.bool_)
    for g in range(N_EXPERT_GROUPS):
        beaten = jnp.zeros((tr, 1), jnp.int32)
        for o in range(N_EXPERT_GROUPS):
            if o == g:
                continue
            wins = (gscore[o] > gscore[g]) | ((gscore[o] == gscore[g]) & (o < g))
            beaten = beaten + wins.astype(jnp.int32)
        keep = keep | ((beaten < TOPK_GROUPS) & (grp == g))

    cur = jnp.where(keep, sel, NEG_INF)
    picked = []
    weights = []
    onehot = jnp.zeros((tr, N_EXPERTS), F32)
    for _ in range(TOP_K):
        m = jnp.max(cur, axis=-1, keepdims=True)
        ik = jnp.min(jnp.where(cur == m, lane_f, float(N_EXPERTS)), axis=-1, keepdims=True)
        hit = lane_f == ik
        weights.append(jnp.sum(jnp.where(hit, scores, 0.0), axis=-1, keepdims=True))
        cur = jnp.where(hit, NEG_INF, cur)
        onehot = onehot + hit.astype(F32)
        picked.append(ik)
    wsum = weights[0]
    for w in weights[1:]:
        wsum = wsum + w

    ri = lax.broadcasted_iota(jnp.int32, (tr, tr), 0)
    ci = lax.broadcasted_iota(jnp.int32, (tr, tr), 1)
    tri = (ri > ci).astype(BF16)
    before = jnp.dot(tri, onehot.astype(BF16), preferred_element_type=F32) + carry[...]
    carry[...] = carry[...] + jnp.sum(onehot, axis=0, keepdims=True)
    cnt_ref[...] = carry[...]

    slot = lax.broadcasted_iota(jnp.int32, (tr, LANES), 1)
    idx_out = jnp.zeros((tr, LANES), jnp.int32)
    rank_out = jnp.zeros((tr, LANES), jnp.int32)
    gw_out = jnp.zeros((tr, LANES), F32)
    for k in range(TOP_K):
        rk = jnp.sum(jnp.where(lane_f == picked[k], before, 0.0), axis=-1, keepdims=True)
        idx_out = jnp.where(slot == k, picked[k].astype(jnp.int32), idx_out)
        rank_out = jnp.where(slot == k, rk.astype(jnp.int32), rank_out)
        gw_out = jnp.where(slot == k, weights[k] / wsum * ROUTED_SCALE, gw_out)
    idx_ref[...] = idx_out
    rank_ref[...] = rank_out
    gw_ref[...] = gw_out


def _router(x1_prompt, x1_sample, wr_hi, wr_lo, b_router):
    d = x1_prompt.shape[1]
    tr = TOKEN_TILE
    npt = x1_prompt.shape[0] // tr
    t = x1_prompt.shape[0] + x1_sample.shape[0]
    full = lambda a: pl.BlockSpec(a.shape, lambda i: (0,) * a.ndim)
    tile = pl.BlockSpec((tr, LANES), lambda i: (i, 0))
    return pl.pallas_call(
        functools.partial(_router_kernel, n_prompt_tiles=npt),
        grid=(t // tr,),
        in_specs=[pl.BlockSpec((tr, d), lambda i: (jnp.minimum(i, npt - 1), 0)),
                  pl.BlockSpec((tr, d), lambda i: (jnp.maximum(i - npt, 0), 0)),
                  full(wr_hi), full(wr_lo), full(b_router)],
        out_specs=[pl.BlockSpec((tr, d), lambda i: (i, 0)), tile, tile, tile,
                   pl.BlockSpec((1, N_EXPERTS), lambda i: (0, 0))],
        out_shape=[jax.ShapeDtypeStruct((t, d), F32),
                   jax.ShapeDtypeStruct((t, LANES), jnp.int32),
                   jax.ShapeDtypeStruct((t, LANES), F32),
                   jax.ShapeDtypeStruct((t, LANES), jnp.int32),
                   jax.ShapeDtypeStruct((1, N_EXPERTS), F32)],
        scratch_shapes=[pltpu.VMEM((1, N_EXPERTS), F32)],
        compiler_params=_cparams("arbitrary"),
        name="router",
    )(x1_prompt, x1_sample, wr_hi, wr_lo, b_router)


TOKEN_CHUNKS = 16
TOKEN_ROWS = TOKEN_CHUNKS + 2
PAD_CHUNKS = (64, 32, 16, 8, 4, 2, 1)


def _pack_rows(x, dst_ref, n_rows):
    for c in range(TOKEN_CHUNKS):
        dst_ref[pl.ds(c, n_rows, stride=TOKEN_ROWS), :] = x[:, c * LANES:(c + 1) * LANES]
    for c in range(TOKEN_CHUNKS, TOKEN_ROWS):
        dst_ref[pl.ds(c, n_rows, stride=TOKEN_ROWS), :] = jnp.zeros((n_rows, LANES), x.dtype)


def _unpack_rows(src_ref, slot0, n_rows):
    row0 = slot0 * TOKEN_ROWS
    return jnp.concatenate([src_ref[pl.ds(row0 + c, n_rows, stride=TOKEN_ROWS), :] for c in range(TOKEN_CHUNKS)],
                           axis=1)


def _slot_rows(slot, n_slots=1):
    return pl.ds(slot * TOKEN_ROWS, n_slots * TOKEN_ROWS)


TOKEN_WORD_ROWS = TOKEN_ROWS // 2


def _word_rows(slot, n_slots=1):
    return pl.ds(slot * TOKEN_WORD_ROWS, n_slots * TOKEN_WORD_ROWS)


def _dispatch_kernel(pstart_ref, cnt_ref, nact_ref, idx_ref, rank_ref, x_ref, xs_hbm, pbuf, zbuf, stage, sem, zsem,
                     *, n_blocks):
    i = pl.program_id(0)
    tt = TOKEN_TILE

    def chunk_fill(slot, p):
        return pltpu.make_async_copy(zbuf.at[_word_rows(0, p), :], xs_hbm.at[_word_rows(slot, p), :], zsem)

    def block_fill(b):
        return chunk_fill(b * MOE_BLOCK, MOE_BLOCK)

    def pad_fills(e, act):
        n = cnt_ref[e]
        slot = pstart_ref[e] + n
        padn = (MOE_BLOCK - (n & (MOE_BLOCK - 1))) & (MOE_BLOCK - 1)
        for p in PAD_CHUNKS:
            @pl.when((padn & p) != 0)
            def _():
                act(chunk_fill(slot, p))
            slot = slot + (padn & p)

    @pl.when(i == 0)
    def _():
        zbuf[...] = jnp.zeros(zbuf.shape, zbuf.dtype)

        def start_e(e, c):
            pad_fills(e, lambda cp: cp.start())
            return c

        def wait_e(e, c):
            pad_fills(e, lambda cp: cp.wait())
            return c

        def start_b(b, c):
            block_fill(b).start()
            return c

        def wait_b(b, c):
            block_fill(b).wait()
            return c

        lax.fori_loop(0, N_EXPERTS, start_e, 0)
        lax.fori_loop(nact_ref[0], n_blocks, start_b, 0)
        lax.fori_loop(0, N_EXPERTS, wait_e, 0)
        lax.fori_loop(nact_ref[0], n_blocks, wait_b, 0)

    _pack_rows(x_ref[...], stage, tt)
    pbuf[...] = pltpu.bitcast(stage[...].astype(BF16), jnp.uint32)

    def token(t, c):
        src = pbuf.at[_word_rows(t), :]
        for j in range(TOP_K):
            a = t * TOP_K + j
            d = pstart_ref[idx_ref[0, 0, a]] + rank_ref[0, 0, a]
            pltpu.make_async_copy(src, xs_hbm.at[_word_rows(d), :], sem).start(priority=j % 2)
        return c

    lax.fori_loop(0, tt, token, 0)
    for j in range(TOP_K):
        pltpu.make_async_copy(pbuf, xs_hbm.at[_word_rows(0, tt), :], sem).wait()


def _dispatch(x1, idx3, rank3, pstart, cnt, nact, n_blocks):
    t, d = x1.shape
    tt = TOKEN_TILE
    assert d == TOKEN_CHUNKS * LANES
    smem_tile = pl.BlockSpec((1, 1, tt * TOP_K), lambda i, *_: (i, 0, 0), memory_space=pltpu.SMEM)
    return pl.pallas_call(
        functools.partial(_dispatch_kernel, n_blocks=n_blocks),
        grid_spec=pltpu.PrefetchScalarGridSpec(
            num_scalar_prefetch=3,
            grid=(t // tt,),
            in_specs=[smem_tile, smem_tile, pl.BlockSpec((tt, d), lambda i, *_: (i, 0))],
            out_specs=pl.BlockSpec(memory_space=pl.ANY),
            scratch_shapes=[pltpu.VMEM((tt * TOKEN_WORD_ROWS, LANES), jnp.uint32),
                            pltpu.VMEM((MOE_BLOCK * TOKEN_WORD_ROWS, LANES), jnp.uint32),
                            pltpu.VMEM((tt * TOKEN_ROWS, LANES), F32),
                            pltpu.SemaphoreType.DMA, pltpu.SemaphoreType.DMA]),
        out_shape=jax.ShapeDtypeStruct((n_blocks * MOE_BLOCK * TOKEN_WORD_ROWS, LANES), jnp.uint32),
        compiler_params=pltpu.CompilerParams(dimension_semantics=("arbitrary",),
                                             vmem_limit_bytes=VMEM_LIMIT, has_side_effects=True),
        name="dispatch",
    )(pstart, cnt, nact, idx3, rank3, x1)


WEIGHT_DMA_PRIORITY = 1
WEIGHT_SLOTS = 3


def _experts_kernel(bexp_ref, nact_ref, xs_ref, wg_hbm, wu_hbm, wd_hbm, y_ref,
                    wg_f32, wu_f32, wd_f32, wg_bf, wu_bf, wd_bf, stage, slot_ref, sems):
    b = pl.program_id(0)
    nact = nact_ref[0]
    last = pl.num_programs(0) - 1
    active = b < nact
    e = bexp_ref[b]
    new_expert = (b == 0) | (e != bexp_ref[jnp.maximum(b - 1, 0)])

    def fetch(expert, slot):
        return (pltpu.make_async_copy(wg_hbm.at[expert], wg_f32.at[slot], sems.at[slot, 0]),
                pltpu.make_async_copy(wu_hbm.at[expert], wu_f32.at[slot], sems.at[slot, 1]),
                pltpu.make_async_copy(wd_hbm.at[expert], wd_f32.at[slot], sems.at[slot, 2]))

    def run_end(start, expert):
        return lax.while_loop(lambda j: (j < nact) & (bexp_ref[jnp.minimum(j, last)] == expert),
                              lambda j: j + 1, start)

    def prefetch(block, slot):
        @pl.when(block < nact)
        def _():
            for cp in fetch(bexp_ref[jnp.minimum(block, last)], slot):
                cp.start(priority=WEIGHT_DMA_PRIORITY)

    @pl.when(active & new_expert)
    def _():
        nxt = run_end(b + 1, e)

        @pl.when(b == 0)
        def _():
            slot_ref[0] = 0
            for cp in fetch(e, 0):
                cp.start(priority=WEIGHT_DMA_PRIORITY)
            prefetch(nxt, 1)

        @pl.when(b > 0)
        def _():
            slot_ref[0] = lax.rem(slot_ref[0] + 1, WEIGHT_SLOTS)

        slot = slot_ref[0]
        nxt2 = run_end(nxt + 1, bexp_ref[jnp.minimum(nxt, last)])
        prefetch(jnp.where(nxt < nact, nxt2, nact), lax.rem(slot + 2, WEIGHT_SLOTS))

        cp_g, cp_u, cp_d = fetch(e, slot)
        cp_g.wait()
        wg_bf[...] = wg_f32[slot].astype(BF16)
        cp_u.wait()
        wu_bf[...] = wu_f32[slot].astype(BF16)
        cp_d.wait()
        wd_bf[...] = wd_f32[slot].astype(BF16)

    @pl.when(active)
    def _():
        stage[...] = pltpu.bitcast(xs_ref[...], BF16).astype(F32)
        x = _unpack_rows(stage, 0, MOE_BLOCK).astype(BF16)
        g = jnp.dot(x, wg_bf[...], preferred_element_type=F32)
        u = jnp.dot(x, wu_bf[...], preferred_element_type=F32)
        h = (_silu(g) * u).astype(BF16)
        _pack_rows(jnp.dot(h, wd_bf[...], preferred_element_type=F32), y_ref, MOE_BLOCK)

    @pl.when(jnp.logical_not(active))
    def _():
        y_ref[...] = jnp.zeros(y_ref.shape, y_ref.dtype)


def _experts(xs, w_gate_e, w_up_e, w_down_e, block_exp, nact):
    n_blocks = xs.shape[0] // (MOE_BLOCK * TOKEN_WORD_ROWS)
    _, d, ff = w_gate_e.shape
    live = lambda b, bexp, nact: (jnp.minimum(b, nact[0] - 1), 0)
    hbm = pl.BlockSpec(memory_space=pl.ANY)
    return pl.pallas_call(
        _experts_kernel,
        grid_spec=pltpu.PrefetchScalarGridSpec(
            num_scalar_prefetch=2,
            grid=(n_blocks,),
            in_specs=[pl.BlockSpec((MOE_BLOCK * TOKEN_WORD_ROWS, LANES), live), hbm, hbm, hbm],
            out_specs=pl.BlockSpec((MOE_BLOCK * TOKEN_ROWS, LANES), lambda b, bexp, nact: (b, 0)),
            scratch_shapes=[pltpu.VMEM((WEIGHT_SLOTS, d, ff), F32), pltpu.VMEM((WEIGHT_SLOTS, d, ff), F32),
                            pltpu.VMEM((WEIGHT_SLOTS, ff, d), F32),
                            pltpu.VMEM((d, ff), BF16), pltpu.VMEM((d, ff), BF16), pltpu.VMEM((ff, d), BF16),
                            pltpu.VMEM((MOE_BLOCK * TOKEN_ROWS, LANES), F32),
                            pltpu.SMEM((1,), jnp.int32), pltpu.SemaphoreType.DMA((WEIGHT_SLOTS, 3))]),
        out_shape=jax.ShapeDtypeStruct((n_blocks * MOE_BLOCK * TOKEN_ROWS, LANES), F32),
        compiler_params=_cparams("arbitrary"),
        name="experts",
    )(block_exp, nact, xs, w_gate_e, w_up_e, w_down_e)


def _combine_kernel(pstart_ref, idx_ref, rank_ref, x_ref, gw_ref, y_hbm, wgs_ref, wus_ref, wds_ref, g_ref, b_ref,
                    op_ref, os_ref, ybuf, sem, *, n_prompt_tiles):
    i = pl.program_id(0)
    tt = TOKEN_TILE

    def token(t, c):
        for j in range(TOP_K):
            a = t * TOP_K + j
            d = pstart_ref[idx_ref[0, 0, a]] + rank_ref[0, 0, a]
            pltpu.make_async_copy(y_hbm.at[_slot_rows(d), :], ybuf.at[_slot_rows(j * tt + t), :],
                                  sem).start(priority=j % 2)
        return c

    lax.fori_loop(0, tt, token, 0)

    x = x_ref[...]
    xb = x.astype(BF16)
    hs = _silu(jnp.dot(xb, wgs_ref[...], preferred_element_type=F32)) * jnp.dot(
        xb, wus_ref[...], preferred_element_type=F32)
    f = jnp.dot(hs.astype(BF16), wds_ref[...], preferred_element_type=F32)

    for j in range(TOP_K):
        pltpu.make_async_copy(y_hbm.at[_slot_rows(0, tt), :], ybuf.at[_slot_rows(j * tt, tt), :], sem).wait()
    gw = gw_ref[...]
    for j in range(TOP_K):
        f = f + gw[:, j:j + 1] * _unpack_rows(ybuf, j * tt, tt)
    out = _layer_norm(DEEPNORM_ALPHA * x + f, g_ref[...], b_ref[...])

    @pl.when(i < n_prompt_tiles)
    def _():
        op_ref[...] = out

    @pl.when(i >= n_prompt_tiles)
    def _():
        os_ref[...] = out


def _combine(x1, idx3, rank3, pstart, gw, y, wgs, wus, wds, ln_g, ln_b, n_prompt_rows):
    t, d = x1.shape
    tt = TOKEN_TILE
    nt = t // tt
    npt = n_prompt_rows // tt
    full = lambda a: pl.BlockSpec(a.shape, lambda i, *_: (0,) * a.ndim)
    smem_tile = pl.BlockSpec((1, 1, tt * TOP_K), lambda i, *_: (i, 0, 0), memory_space=pltpu.SMEM)
    return pl.pallas_call(
        functools.partial(_combine_kernel, n_prompt_tiles=npt),
        grid_spec=pltpu.PrefetchScalarGridSpec(
            num_scalar_prefetch=1,
            grid=(nt,),
            in_specs=[smem_tile, smem_tile,
                      pl.BlockSpec((tt, d), lambda i, *_: (i, 0)),
                      pl.BlockSpec((tt, LANES), lambda i, *_: (i, 0)),
                      pl.BlockSpec(memory_space=pl.ANY),
                      full(wgs), full(wus), full(wds), full(ln_g), full(ln_b)],
            out_specs=[pl.BlockSpec((tt, d), lambda i, *_: (jnp.minimum(i, npt - 1), 0)),
                       pl.BlockSpec((tt, d), lambda i, *_: (jnp.maximum(i - npt, 0), 0))],
            scratch_shapes=[pltpu.VMEM((TOP_K * tt * TOKEN_ROWS, LANES), F32), pltpu.SemaphoreType.DMA]),
        out_shape=[jax.ShapeDtypeStruct((n_prompt_rows, d), F32),
                   jax.ShapeDtypeStruct((t - n_prompt_rows, d), F32)],
        compiler_params=_cparams("arbitrary"),
        name="combine",
    )(pstart, idx3, rank3, x1, gw, y, wgs, wus, wds, ln_g, ln_b)


def _moe_plan(idx, rank, counts, n_tokens):
    cnt = counts[0].astype(jnp.int32)
    padded = ((cnt + MOE_BLOCK - 1) // MOE_BLOCK) * MOE_BLOCK
    pend = jnp.cumsum(padded)
    pstart = pend - padded
    n_assign = n_tokens * TOP_K
    n_blocks = (n_assign + N_EXPERTS * (MOE_BLOCK - 1)) // MOE_BLOCK
    tile3 = lambda a: a[:, :TOP_K].reshape(n_tokens // TOKEN_TILE, 1, TOKEN_TILE * TOP_K)
    nact = (pend[-1:] // MOE_BLOCK).astype(jnp.int32)
    block_start = jnp.arange(n_blocks, dtype=jnp.int32) * MOE_BLOCK
    block_exp = jnp.minimum(jnp.sum(pend[None, :] <= block_start[:, None], axis=1), N_EXPERTS - 1)
    return tile3(idx), tile3(rank), pstart.astype(jnp.int32), cnt, nact, block_exp.astype(jnp.int32), n_blocks


def kernel(x_prompt, x_sample, cache_k, cache_v, state_conv, w_in, sinks, w_attn_proj, conv_w, conv_b,
           conv_ln_g, conv_ln_b, w_conv_proj, w_out, ln1_g, ln1_b, w_router, b_router, w_gate_e, w_up_e,
           w_down_e, w_gate_s, w_up_s, w_down_s, ln2_g, ln2_b):
    assert w_in.shape[0] == DEPTH
    batch, seq, d = x_prompt.shape
    nbatch = x_sample.shape[0]
    n_prompt = batch * seq
    n_tokens = n_prompt + nbatch
    ch = conv_w.shape[2]
    row = lambda a: a[0].reshape(1, -1)

    w_in_bf = w_in[0].astype(BF16)
    wap = w_attn_proj[0].astype(BF16)
    wcp = w_conv_proj[0].astype(BF16)
    wout = w_out[0].astype(BF16)
    wgs = w_gate_s[0].astype(BF16)
    wus = w_up_s[0].astype(BF16)
    wds = w_down_s[0].astype(BF16)
    wr = w_router[0].astype(F32)
    wr_hi = wr.astype(BF16)
    wr_lo = (wr - wr_hi.astype(F32)).astype(BF16)

    xp2d = x_prompt.reshape(n_prompt, d)
    xs2d = x_sample.reshape(nbatch, d)

    zp = _inproj(xp2d, w_in_bf, tm=1024, tn=1536)
    o_p = _attn_prompt(zp, sinks[0], batch, seq)
    c_p, newconv_p = _conv_prompt(zp, conv_w[0], row(conv_b), row(conv_ln_g), row(conv_ln_b), batch, seq, ts=256)
    x1_p = _merge(xp2d, o_p, c_p, zp, wap, wcp, wout, row(ln1_g), row(ln1_b), tm=256)
    k_off, v_off, ca_off = ATTN_WIDTH, ATTN_WIDTH + KV_WIDTH, ATTN_WIDTH + 2 * KV_WIDTH
    zp3 = zp.reshape(batch, seq, -1)
    newk_p = zp3[:, seq - WINDOW:, k_off:v_off].reshape(1, batch, WINDOW, N_KV_HEADS, HEAD_DIM)
    newv_p = zp3[:, seq - WINDOW:, v_off:ca_off].reshape(1, batch, WINDOW, N_KV_HEADS, HEAD_DIM)

    zs = _inproj(xs2d, w_in_bf, tm=nbatch, tn=1536)
    own_block = (jnp.arange(KV_WIDTH)[None, :] // HEAD_DIM) == (jnp.arange(N_HEADS)[:, None] // GQA_GROUP)
    q3 = jnp.where(own_block[None], jnp.tile(zs[:, :k_off].reshape(nbatch, N_HEADS, HEAD_DIM), (1, 1, N_KV_HEADS)), 0.0)
    kn3 = zs[:, k_off:v_off].reshape(nbatch, 1, KV_WIDTH)
    vn3 = zs[:, v_off:ca_off].reshape(nbatch, 1, KV_WIDTH)
    slopes_col = jnp.asarray(ALIBI_SLOPES, F32).reshape(N_HEADS, 1)
    o_s3, newk_s, newv_s = _attn_sample(
        q3, kn3, vn3, cache_k[0].reshape(nbatch, WINDOW, KV_WIDTH), cache_v[0].reshape(nbatch, WINDOW, KV_WIDTH),
        sinks[0].reshape(N_HEADS, 1), slopes_col, bb=8)
    c_s, newconv_s = _conv_sample(state_conv[0], zs[:, ca_off:ca_off + ch], zs[:, ca_off + ch:ca_off + 2 * ch],
                                  conv_w[0], row(conv_b), row(conv_ln_g), row(conv_ln_b), bb=8)
    x1_s = _merge(xs2d, o_s3.reshape(nbatch, ATTN_WIDTH), c_s, zs, wap, wcp, wout, row(ln1_g), row(ln1_b),
                  tm=nbatch)

    x1, idx, gw, rank, counts = _router(x1_p, x1_s, wr_hi, wr_lo, row(b_router))
    idx3, rank3, pstart, cnt, nact, block_exp, n_blocks = _moe_plan(idx, rank, counts, n_tokens)
    xs_sorted = _dispatch(x1, idx3, rank3, pstart, cnt, nact, n_blocks)
    y_sorted = _experts(xs_sorted, w_gate_e[0], w_up_e[0], w_down_e[0], block_exp, nact)
    y_p, y_s = _combine(x1, idx3, rank3, pstart, gw, y_sorted, wgs, wus, wds, row(ln2_g), row(ln2_b), n_prompt)

    return (y_p.reshape(batch, seq, d), y_s.reshape(nbatch, 1, d),
            newk_p, newv_p, newconv_p.reshape(1, batch, CONV_WIDTH - 1, ch),
            newk_s.reshape(1, nbatch, WINDOW, N_KV_HEADS, HEAD_DIM),
            newv_s.reshape(1, nbatch, WINDOW, N_KV_HEADS, HEAD_DIM),
            newconv_s.reshape(1, nbatch, CONV_WIDTH - 1, ch))
```

```python
import functools

import jax
import jax.numpy as jnp
from jax import lax
from jax.experimental import pallas as pl
from jax.experimental.pallas import tpu as pltpu

HEAD_DIM = 64
N_HEADS = 16
N_KV_HEADS = 4
GQA_GROUP = N_HEADS // N_KV_HEADS
WINDOW = 128
ATTN_WIDTH = N_HEADS * HEAD_DIM
KV_WIDTH = N_KV_HEADS * HEAD_DIM
CONV_WIDTH = 31
N_EXPERTS = 256
TOP_K = 8
N_EXPERT_GROUPS = 8
GROUP_SIZE = N_EXPERTS // N_EXPERT_GROUPS
TOPK_GROUPS = 4
ROUTED_SCALE = 2.5
LN_EPS = 1e-5
DEPTH = 1
DEEPNORM_ALPHA = (2 * DEPTH) ** 0.25
ALIBI_SLOPES = tuple(2.0 ** (-8.0 * (h + 1.0) / N_HEADS) for h in range(N_HEADS))
ATTN_SCALE = HEAD_DIM ** -0.5

MOE_BLOCK = 128
TOKEN_TILE = 128
LANES = 128
SUBLANES = 8
VMEM_LIMIT = 56 * 1024 * 1024

BF16 = jnp.bfloat16
F32 = jnp.float32
NEG_INF = float("-inf")


def _cparams(*sem):
    return pltpu.CompilerParams(dimension_semantics=sem, vmem_limit_bytes=VMEM_LIMIT)


def _sigmoid(x):
    return 1.0 / (1.0 + jnp.exp(-x))


def _silu(x):
    return x * _sigmoid(x)


def _layer_norm(x, g, b):
    mu = jnp.mean(x, axis=-1, keepdims=True)
    xc = x - mu
    var = jnp.mean(xc * xc, axis=-1, keepdims=True)
    return xc * lax.rsqrt(var + LN_EPS) * g + b


def _inproj_kernel(x_ref, w_ref, o_ref, xb_ref):
    @pl.when(pl.program_id(1) == 0)
    def _():
        xb_ref[...] = x_ref[...].astype(BF16)

    o_ref[...] = jnp.dot(xb_ref[...], w_ref[...], preferred_element_type=F32)


def _inproj(x2d, w_bf, tm, tn):
    m, k = x2d.shape
    n = w_bf.shape[1]
    return pl.pallas_call(
        _inproj_kernel,
        grid=(m // tm, n // tn),
        in_specs=[pl.BlockSpec((tm, k), lambda i, j: (i, 0)),
                  pl.BlockSpec((k, tn), lambda i, j: (0, j))],
        out_specs=pl.BlockSpec((tm, tn), lambda i, j: (i, j)),
        out_shape=jax.ShapeDtypeStruct((m, n), F32),
        scratch_shapes=[pltpu.VMEM((tm, k), BF16)],
        compiler_params=_cparams("parallel", "arbitrary"),
        name="inproj",
    )(x2d, w_bf)


def _attn_prompt_kernel(sinks_ref, q_ref, kp_ref, kc_ref, vp_ref, vc_ref, o_ref):
    i = pl.program_id(1)
    q = q_ref[...]
    k = jnp.concatenate([kp_ref[...], kc_ref[...]], axis=0).astype(BF16)
    v = jnp.concatenate([vp_ref[...], vc_ref[...]], axis=0).astype(BF16)
    qi = lax.broadcasted_iota(jnp.int32, (WINDOW, 2 * WINDOW), 0)
    kj = lax.broadcasted_iota(jnp.int32, (WINDOW, 2 * WINDOW), 1)
    dist = qi + WINDOW - kj
    valid = (dist >= 0) & (dist <= WINDOW) & ((kj >= WINDOW) | (i > 0))
    distf = dist.astype(F32)
    outs = []
    for h in range(N_HEADS):
        kv = h // GQA_GROUP
        qh = q[:, h * HEAD_DIM:(h + 1) * HEAD_DIM].astype(BF16)
        kh = k[:, kv * HEAD_DIM:(kv + 1) * HEAD_DIM]
        vh = v[:, kv * HEAD_DIM:(kv + 1) * HEAD_DIM]
        s = lax.dot_general(qh, kh, (((1,), (1,)), ((), ())), preferred_element_type=F32) * ATTN_SCALE
        s = jnp.where(valid, s - ALIBI_SLOPES[h] * distf, NEG_INF)
        sink = sinks_ref[h]
        m = jnp.maximum(jnp.max(s, axis=-1, keepdims=True), sink)
        e = jnp.exp(s - m)
        denom = jnp.sum(e, axis=-1, keepdims=True) + jnp.exp(sink - m)
        p = (e / denom).astype(BF16)
        outs.append(jnp.dot(p, vh, preferred_element_type=F32))
    o_ref[...] = jnp.concatenate(outs, axis=-1).astype(o_ref.dtype)


def _attn_prompt(z2d, sinks, batch, seq):
    nb = seq // WINDOW
    kcol = ATTN_WIDTH // KV_WIDTH
    vcol = kcol + 1

    def cur(col):
        return lambda b, i: (b * nb + i, col)

    def prev(col):
        return lambda b, i: (b * nb + jnp.maximum(i - 1, 0), col)

    return pl.pallas_call(
        _attn_prompt_kernel,
        grid=(batch, nb),
        in_specs=[pl.BlockSpec(memory_space=pltpu.SMEM),
                  pl.BlockSpec((WINDOW, ATTN_WIDTH), cur(0)),
                  pl.BlockSpec((WINDOW, KV_WIDTH), prev(kcol)),
                  pl.BlockSpec((WINDOW, KV_WIDTH), cur(kcol)),
                  pl.BlockSpec((WINDOW, KV_WIDTH), prev(vcol)),
                  pl.BlockSpec((WINDOW, KV_WIDTH), cur(vcol))],
        out_specs=pl.BlockSpec((WINDOW, ATTN_WIDTH), cur(0)),
        out_shape=jax.ShapeDtypeStruct((batch * seq, ATTN_WIDTH), BF16),
        compiler_params=_cparams("parallel", "parallel"),
        name="attn_prompt",
    )(sinks, z2d, z2d, z2d, z2d, z2d)


def _attn_sample_kernel(sinks_ref, slopes_ref, q_ref, kn_ref, vn_ref, ck_ref, cv_ref,
                        o_ref, nk_ref, nv_ref, *, bb):
    dist_c = (WINDOW - lax.broadcasted_iota(jnp.int32, (1, WINDOW), 1)).astype(F32)
    sink = sinks_ref[...]
    bias_c = slopes_ref[...] * dist_c
    head = lax.broadcasted_iota(jnp.int32, (N_HEADS, KV_WIDTH), 0)
    lane = lax.broadcasted_iota(jnp.int32, (N_HEADS, KV_WIDTH), 1)
    own = (lane // HEAD_DIM) == (head // GQA_GROUP)
    for b in range(bb):
        qb = q_ref[b].astype(BF16)
        kc = ck_ref[b]
        vc = cv_ref[b]
        kn = kn_ref[b]
        vn = vn_ref[b]
        sc = lax.dot_general(qb, kc.astype(BF16), (((1,), (1,)), ((), ())),
                             preferred_element_type=F32) * ATTN_SCALE - bias_c
        sn = jnp.sum(qb.astype(F32) * kn.astype(BF16).astype(F32), axis=-1, keepdims=True) * ATTN_SCALE
        m = jnp.maximum(jnp.maximum(jnp.max(sc, axis=-1, keepdims=True), sn), sink)
        ec = jnp.exp(sc - m)
        en = jnp.exp(sn - m)
        denom = jnp.sum(ec, axis=-1, keepdims=True) + en + jnp.exp(sink - m)
        pc = (ec / denom).astype(BF16)
        pn = (en / denom).astype(BF16).astype(F32)
        r = jnp.dot(pc, vc.astype(BF16), preferred_element_type=F32) + pn * vn.astype(BF16).astype(F32)
        r = jnp.where(own, r, 0.0)
        o = r[:, 0:HEAD_DIM]
        for kv in range(1, N_KV_HEADS):
            o = o + r[:, kv * HEAD_DIM:(kv + 1) * HEAD_DIM]
        o_ref[b] = o.astype(o_ref.dtype)
        nk_ref[b, 0:WINDOW - 1, :] = kc[1:WINDOW, :]
        nk_ref[b, WINDOW - 1:WINDOW, :] = kn
        nv_ref[b, 0:WINDOW - 1, :] = vc[1:WINDOW, :]
        nv_ref[b, WINDOW - 1:WINDOW, :] = vn


def _attn_sample(q3, kn3, vn3, ck, cv, sinks_col, slopes_col, bb):
    nbatch = q3.shape[0]
    blk3 = lambda s1, s2: pl.BlockSpec((bb, s1, s2), lambda i: (i, 0, 0))
    full = lambda a: pl.BlockSpec(a.shape, lambda i: (0,) * a.ndim)
    return pl.pallas_call(
        functools.partial(_attn_sample_kernel, bb=bb),
        grid=(nbatch // bb,),
        in_specs=[full(sinks_col), full(slopes_col),
                  blk3(N_HEADS, KV_WIDTH), blk3(1, KV_WIDTH), blk3(1, KV_WIDTH),
                  blk3(WINDOW, KV_WIDTH), blk3(WINDOW, KV_WIDTH)],
        out_specs=[blk3(N_HEADS, HEAD_DIM), blk3(WINDOW, KV_WIDTH), blk3(WINDOW, KV_WIDTH)],
        out_shape=[jax.ShapeDtypeStruct((nbatch, N_HEADS, HEAD_DIM), BF16),
                   jax.ShapeDtypeStruct((nbatch, WINDOW, KV_WIDTH), F32),
                   jax.ShapeDtypeStruct((nbatch, WINDOW, KV_WIDTH), F32)],
        compiler_params=_cparams("parallel"),
        name="attn_sample",
    )(sinks_col, slopes_col, q3, kn3, vn3, ck, cv)


CONV_HALO = 32
CONV_ROWS = 64


def _conv_prompt_kernel(ca0, ca1, cb0, cb1, w_ref, b_ref, g_ref, beta_ref,
                        c_ref, nc_ref, ext, shifted, ybuf, *, ts):
    i = pl.program_id(1)
    half = ca0.shape[1]
    span = shifted.shape[1]

    @pl.when(i == 0)
    def _():
        ext[0:CONV_HALO, :] = jnp.zeros((CONV_HALO, ext.shape[1]), F32)

    @pl.when(i > 0)
    def _():
        ext[0:CONV_HALO, :] = ext[ts:ts + CONV_HALO, :]

    ext[CONV_HALO:CONV_HALO + ts, 0:half] = ca0[...] * _sigmoid(cb0[...])
    ext[CONV_HALO:CONV_HALO + ts, half:2 * half] = ca1[...] * _sigmoid(cb1[...])

    for q in range(1, SUBLANES):
        shifted[q - 1] = ext[q:q + span, :]

    shift = CONV_HALO - (CONV_WIDTH - 1)
    for r in range(ts // CONV_ROWS):
        for c in range(ext.shape[1] // LANES):
            cols = slice(c * LANES, (c + 1) * LANES)
            acc = jnp.zeros((CONV_ROWS, LANES), F32)
            for j in range(CONV_WIDTH):
                q = (shift + j) % SUBLANES
                r0 = r * CONV_ROWS + shift + j - q
                rows = ext[r0:r0 + CONV_ROWS, cols] if q == 0 else shifted[q - 1, r0:r0 + CONV_ROWS, cols]
                acc = acc + w_ref[j:j + 1, cols] * rows
            ybuf[r * CONV_ROWS:(r + 1) * CONV_ROWS, cols] = acc

    y = ybuf[...] + b_ref[...]
    c_ref[...] = _silu(_layer_norm(y, g_ref[...], beta_ref[...])).astype(c_ref.dtype)

    @pl.when(i == pl.num_programs(1) - 1)
    def _():
        nc_ref[...] = ext[CONV_HALO + ts - (CONV_WIDTH - 1):CONV_HALO + ts, :]


def _conv_prompt(z2d, conv_w, conv_b, ln_g, ln_b, batch, seq, ts):
    nt = seq // ts
    ch = conv_w.shape[1]
    half = ch // 2
    ca_col = (ATTN_WIDTH + 2 * KV_WIDTH) // half
    cb_col = ca_col + 2
    zblk = lambda col: pl.BlockSpec((ts, half), lambda b, i: (b * nt + i, col))
    full = lambda a: pl.BlockSpec(a.shape, lambda b, i: (0,) * a.ndim)
    return pl.pallas_call(
        functools.partial(_conv_prompt_kernel, ts=ts),
        grid=(batch, nt),
        in_specs=[zblk(ca_col), zblk(ca_col + 1), zblk(cb_col), zblk(cb_col + 1),
                  full(conv_w), full(conv_b), full(ln_g), full(ln_b)],
        out_specs=[pl.BlockSpec((ts, ch), lambda b, i: (b * nt + i, 0)),
                   pl.BlockSpec((None, CONV_WIDTH - 1, ch), lambda b, i: (b, 0, 0))],
        out_shape=[jax.ShapeDtypeStruct((batch * seq, ch), BF16),
                   jax.ShapeDtypeStruct((batch, CONV_WIDTH - 1, ch), F32)],
        scratch_shapes=[pltpu.VMEM((CONV_HALO + ts, ch), F32),
                        pltpu.VMEM((SUBLANES - 1, CONV_HALO + ts - SUBLANES, ch), F32),
                        pltpu.VMEM((ts, ch), F32)],
        compiler_params=_cparams("parallel", "arbitrary"),
        name="conv_prompt",
    )(z2d, z2d, z2d, z2d, conv_w, conv_b, ln_g, ln_b)


def _conv_sample_kernel(st_ref, ca_ref, cb_ref, w_ref, b_ref, g_ref, beta_ref, c_ref, ns_ref, *, bb):
    u = ca_ref[...] * _sigmoid(cb_ref[...])
    w_past = w_ref[0:CONV_WIDTH - 1, :]
    w_new = w_ref[CONV_WIDTH - 1:CONV_WIDTH, :]
    for b in range(bb):
        st = st_ref[b]
        ub = u[b:b + 1, :]
        y = jnp.sum(st * w_past, axis=0, keepdims=True) + ub * w_new + b_ref[...]
        c_ref[b:b + 1, :] = _silu(_layer_norm(y, g_ref[...], beta_ref[...])).astype(c_ref.dtype)
        ns_ref[b, 0:CONV_WIDTH - 2, :] = st[1:CONV_WIDTH - 1, :]
        ns_ref[b, CONV_WIDTH - 2:CONV_WIDTH - 1, :] = ub


def _conv_sample(state, ca, cb, conv_w, conv_b, ln_g, ln_b, bb):
    nbatch, npast, ch = state.shape
    full = lambda a: pl.BlockSpec(a.shape, lambda i: (0,) * a.ndim)
    return pl.pallas_call(
        functools.partial(_conv_sample_kernel, bb=bb),
        grid=(nbatch // bb,),
        in_specs=[pl.BlockSpec((bb, npast, ch), lambda i: (i, 0, 0)),
                  pl.BlockSpec((bb, ch), lambda i: (i, 0)),
                  pl.BlockSpec((bb, ch), lambda i: (i, 0)),
                  full(conv_w), full(conv_b), full(ln_g), full(ln_b)],
        out_specs=[pl.BlockSpec((bb, ch), lambda i: (i, 0)),
                   pl.BlockSpec((bb, npast, ch), lambda i: (i, 0, 0))],
        out_shape=[jax.ShapeDtypeStruct((nbatch, ch), BF16),
                   jax.ShapeDtypeStruct((nbatch, npast, ch), F32)],
        compiler_params=_cparams("parallel"),
        name="conv_sample",
    )(state, ca, cb, conv_w, conv_b, ln_g, ln_b)


def _merge_kernel(*refs, n_gate_blocks):
    x_ref, o_ref, c_ref = refs[0:3]
    ga_refs = refs[3:3 + n_gate_blocks]
    gc_refs = refs[3 + n_gate_blocks:3 + 2 * n_gate_blocks]
    wap_ref, wcp_ref, wout_ref, g_ref, b_ref, out_ref = refs[3 + 2 * n_gate_blocks:]
    a = jnp.dot(o_ref[...], wap_ref[...], preferred_element_type=F32)
    cc = jnp.dot(c_ref[...], wcp_ref[...], preferred_element_type=F32)
    ga = jnp.concatenate([r[...] for r in ga_refs], axis=-1)
    gc = jnp.concatenate([r[...] for r in gc_refs], axis=-1)
    mixed = _sigmoid(ga) * a + _sigmoid(gc) * cc
    y = jnp.dot(mixed.astype(BF16), wout_ref[...], preferred_element_type=F32)
    out_ref[...] = _layer_norm(DEEPNORM_ALPHA * x_ref[...] + y, g_ref[...], b_ref[...])


def _merge(x2d, o2d, c2d, z2d, wap, wcp, wout, ln_g, ln_b, tm):
    m, d = x2d.shape
    gw = 512
    n_gate_blocks = d // gw
    ga_col = (z2d.shape[1] - 2 * d) // gw
    gc_col = ga_col + n_gate_blocks
    row = lambda w: pl.BlockSpec((tm, w), lambda i: (i, 0))
    zblk = lambda col: pl.BlockSpec((tm, gw), lambda i: (i, col))
    full = lambda a: pl.BlockSpec(a.shape, lambda i: (0,) * a.ndim)
    in_specs = ([row(d), row(o2d.shape[1]), row(c2d.shape[1])]
                + [zblk(ga_col + k) for k in range(n_gate_blocks)]
                + [zblk(gc_col + k) for k in range(n_gate_blocks)]
                + [full(wap), full(wcp), full(wout), full(ln_g), full(ln_b)])
    args = [x2d, o2d, c2d] + [z2d] * (2 * n_gate_blocks) + [wap, wcp, wout, ln_g, ln_b]
    return pl.pallas_call(
        functools.partial(_merge_kernel, n_gate_blocks=n_gate_blocks),
        grid=(m // tm,),
        in_specs=in_specs,
        out_specs=pl.BlockSpec((tm, d), lambda i: (i, 0)),
        out_shape=jax.ShapeDtypeStruct((m, d), F32),
        compiler_params=_cparams("parallel"),
        name="merge",
    )(*args)


def _router_kernel(xp_ref, xs_ref, wh_ref, wl_ref, b_ref, x_ref, idx_ref, gw_ref, rank_ref, cnt_ref, carry,
                   *, n_prompt_tiles):
    @pl.when(pl.program_id(0) == 0)
    def _():
        carry[...] = jnp.zeros(carry.shape, F32)

    x = jnp.where(pl.program_id(0) < n_prompt_tiles, xp_ref[...], xs_ref[...])
    x_ref[...] = x
    tr = x.shape[0]
    xh = x.astype(BF16)
    xl = (x - xh.astype(F32)).astype(BF16)
    wh = wh_ref[...]
    logits = (jnp.dot(xh, wh, preferred_element_type=F32)
              + jnp.dot(xl, wh, preferred_element_type=F32)
              + jnp.dot(xh, wl_ref[...], preferred_element_type=F32))
    scores = _sigmoid(logits)
    sel = scores + b_ref[...]
    lane = lax.broadcasted_iota(jnp.int32, (tr, N_EXPERTS), 1)
    lane_f = lane.astype(F32)
    grp = lane // GROUP_SIZE

    gscore = []
    for g in range(N_EXPERT_GROUPS):
        v = jnp.where(grp == g, sel, NEG_INF)
        m1 = jnp.max(v, axis=-1, keepdims=True)
        is_max = v == m1
        n_max = jnp.sum(is_max.astype(F32), axis=-1, keepdims=True)
        below = jnp.max(jnp.where(is_max, NEG_INF, v), axis=-1, keepdims=True)
        gscore.append(m1 + jnp.where(n_max > 1.5, m1, below))

    keep = jnp.zeros((tr, N_EXPERTS), jnp.bool_)
    for g in range(N_EXPERT_GROUPS):
        beaten = jnp.zeros((tr, 1), jnp.int32)
        for o in range(N_EXPERT_GROUPS):
            if o == g:
                continue
            wins = (gscore[o] > gscore[g]) | ((gscore[o] == gscore[g]) & (o < g))
            beaten = beaten + wins.astype(jnp.int32)
        keep = keep | ((beaten < TOPK_GROUPS) & (grp == g))

    cur = jnp.where(keep, sel, NEG_INF)
    picked = []
    weights = []
    onehot = jnp.zeros((tr, N_EXPERTS), F32)
    for _ in range(TOP_K):
        m = jnp.max(cur, axis=-1, keepdims=True)
        ik = jnp.min(jnp.where(cur == m, lane_f, float(N_EXPERTS)), axis=-1, keepdims=True)
        hit = lane_f == ik
        weights.append(jnp.sum(jnp.where(hit, scores, 0.0), axis=-1, keepdims=True))
        cur = jnp.where(hit, NEG_INF, cur)
        onehot = onehot + hit.astype(F32)
        picked.append(ik)
    wsum = weights[0]
    for w in weights[1:]:
        wsum = wsum + w

    ri = lax.broadcasted_iota(jnp.int32, (tr, tr), 0)
    ci = lax.broadcasted_iota(jnp.int32, (tr, tr), 1)
    tri = (ri > ci).astype(BF16)
    before = jnp.dot(tri, onehot.astype(BF16), preferred_element_type=F32) + carry[...]
    carry[...] = carry[...] + jnp.sum(onehot, axis=0, keepdims=True)
    cnt_ref[...] = carry[...]

    slot = lax.broadcasted_iota(jnp.int32, (tr, LANES), 1)
    idx_out = jnp.zeros((tr, LANES), jnp.int32)
    rank_out = jnp.zeros((tr, LANES), jnp.int32)
    gw_out = jnp.zeros((tr, LANES), F32)
    for k in range(TOP_K):
        rk = jnp.sum(jnp.where(lane_f == picked[k], before, 0.0), axis=-1, keepdims=True)
        idx_out = jnp.where(slot == k, picked[k].astype(jnp.int32), idx_out)
        rank_out = jnp.where(slot == k, rk.astype(jnp.int32), rank_out)
        gw_out = jnp.where(slot == k, weights[k] / wsum * ROUTED_SCALE, gw_out)
    idx_ref[...] = idx_out
    rank_ref[...] = rank_out
    gw_ref[...] = gw_out


def _router(x1_prompt, x1_sample, wr_hi, wr_lo, b_router):
    d = x1_prompt.shape[1]
    tr = TOKEN_TILE
    npt = x1_prompt.shape[0] // tr
    t = x1_prompt.shape[0] + x1_sample.shape[0]
    full = lambda a: pl.BlockSpec(a.shape, lambda i: (0,) * a.ndim)
    tile = pl.BlockSpec((tr, LANES), lambda i: (i, 0))
    return pl.pallas_call(
        functools.partial(_router_kernel, n_prompt_tiles=npt),
        grid=(t // tr,),
        in_specs=[pl.BlockSpec((tr, d), lambda i: (jnp.minimum(i, npt - 1), 0)),
                  pl.BlockSpec((tr, d), lambda i: (jnp.maximum(i - npt, 0), 0)),
                  full(wr_hi), full(wr_lo), full(b_router)],
        out_specs=[pl.BlockSpec((tr, d), lambda i: (i, 0)), tile, tile, tile,
                   pl.BlockSpec((1, N_EXPERTS), lambda i: (0, 0))],
        out_shape=[jax.ShapeDtypeStruct((t, d), F32),
                   jax.ShapeDtypeStruct((t, LANES), jnp.int32),
                   jax.ShapeDtypeStruct((t, LANES), F32),
                   jax.ShapeDtypeStruct((t, LANES), jnp.int32),
                   jax.ShapeDtypeStruct((1, N_EXPERTS), F32)],
        scratch_shapes=[pltpu.VMEM((1, N_EXPERTS), F32)],
        compiler_params=_cparams("arbitrary"),
        name="router",
    )(x1_prompt, x1_sample, wr_hi, wr_lo, b_router)


TOKEN_CHUNKS = 16
TOKEN_ROWS = TOKEN_CHUNKS + 2
PAD_CHUNKS = (64, 32, 16, 8, 4, 2, 1)


def _pack_rows(x, dst_ref, n_rows):
    for c in range(TOKEN_CHUNKS):
        dst_ref[pl.ds(c, n_rows, stride=TOKEN_ROWS), :] = x[:, c * LANES:(c + 1) * LANES]
    for c in range(TOKEN_CHUNKS, TOKEN_ROWS):
        dst_ref[pl.ds(c, n_rows, stride=TOKEN_ROWS), :] = jnp.zeros((n_rows, LANES), x.dtype)


def _unpack_rows(src_ref, slot0, n_rows):
    row0 = slot0 * TOKEN_ROWS
    return jnp.concatenate([src_ref[pl.ds(row0 + c, n_rows, stride=TOKEN_ROWS), :] for c in range(TOKEN_CHUNKS)],
                           axis=1)


def _slot_rows(slot, n_slots=1):
    return pl.ds(slot * TOKEN_ROWS, n_slots * TOKEN_ROWS)


TOKEN_WORD_ROWS = TOKEN_ROWS // 2


def _word_rows(slot, n_slots=1):
    return pl.ds(slot * TOKEN_WORD_ROWS, n_slots * TOKEN_WORD_ROWS)


def _dispatch_kernel(pstart_ref, cnt_ref, nact_ref, idx_ref, rank_ref, x_ref, xs_hbm, pbuf, zbuf, stage, sem, zsem,
                     *, n_blocks):
    i = pl.program_id(0)
    tt = TOKEN_TILE

    def chunk_fill(slot, p):
        return pltpu.make_async_copy(zbuf.at[_word_rows(0, p), :], xs_hbm.at[_word_rows(slot, p), :], zsem)

    def block_fill(b):
        return chunk_fill(b * MOE_BLOCK, MOE_BLOCK)

    def pad_fills(e, act):
        n = cnt_ref[e]
        slot = pstart_ref[e] + n
        padn = (MOE_BLOCK - (n & (MOE_BLOCK - 1))) & (MOE_BLOCK - 1)
        for p in PAD_CHUNKS:
            @pl.when((padn & p) != 0)
            def _():
                act(chunk_fill(slot, p))
            slot = slot + (padn & p)

    @pl.when(i == 0)
    def _():
        zbuf[...] = jnp.zeros(zbuf.shape, zbuf.dtype)

        def start_e(e, c):
            pad_fills(e, lambda cp: cp.start())
            return c

        def wait_e(e, c):
            pad_fills(e, lambda cp: cp.wait())
            return c

        def start_b(b, c):
            block_fill(b).start()
            return c

        def wait_b(b, c):
            block_fill(b).wait()
            return c

        lax.fori_loop(0, N_EXPERTS, start_e, 0)
        lax.fori_loop(nact_ref[0], n_blocks, start_b, 0)
        lax.fori_loop(0, N_EXPERTS, wait_e, 0)
        lax.fori_loop(nact_ref[0], n_blocks, wait_b, 0)

    _pack_rows(x_ref[...], stage, tt)
    pbuf[...] = pltpu.bitcast(stage[...].astype(BF16), jnp.uint32)

    def token(t, c):
        src = pbuf.at[_word_rows(t), :]
        for j in range(TOP_K):
            a = t * TOP_K + j
            d = pstart_ref[idx_ref[0, 0, a]] + rank_ref[0, 0, a]
            pltpu.make_async_copy(src, xs_hbm.at[_word_rows(d), :], sem).start(priority=j % 2)
        return c

    lax.fori_loop(0, tt, token, 0)
    for j in range(TOP_K):
        pltpu.make_async_copy(pbuf, xs_hbm.at[_word_rows(0, tt), :], sem).wait()


def _dispatch(x1, idx3, rank3, pstart, cnt, nact, n_blocks):
    t, d = x1.shape
    tt = TOKEN_TILE
    assert d == TOKEN_CHUNKS * LANES
    smem_tile = pl.BlockSpec((1, 1, tt * TOP_K), lambda i, *_: (i, 0, 0), memory_space=pltpu.SMEM)
    return pl.pallas_call(
        functools.partial(_dispatch_kernel, n_blocks=n_blocks),
        grid_spec=pltpu.PrefetchScalarGridSpec(
            num_scalar_prefetch=3,
            grid=(t // tt,),
            in_specs=[smem_tile, smem_tile, pl.BlockSpec((tt, d), lambda i, *_: (i, 0))],
            out_specs=pl.BlockSpec(memory_space=pl.ANY),
            scratch_shapes=[pltpu.VMEM((tt * TOKEN_WORD_ROWS, LANES), jnp.uint32),
                            pltpu.VMEM((MOE_BLOCK * TOKEN_WORD_ROWS, LANES), jnp.uint32),
                            pltpu.VMEM((tt * TOKEN_ROWS, LANES), F32),
                            pltpu.SemaphoreType.DMA, pltpu.SemaphoreType.DMA]),
        out_shape=jax.ShapeDtypeStruct((n_blocks * MOE_BLOCK * TOKEN_WORD_ROWS, LANES), jnp.uint32),
        compiler_params=pltpu.CompilerParams(dimension_semantics=("arbitrary",),
                                             vmem_limit_bytes=VMEM_LIMIT, has_side_effects=True),
        name="dispatch",
    )(pstart, cnt, nact, idx3, rank3, x1)


WEIGHT_DMA_PRIORITY = 1
WEIGHT_SLOTS = 3


XS_RING = 3


def _experts_kernel(bexp_ref, nact_ref, xs_hbm, wg_hbm, wu_hbm, wd_hbm, y_ref,
                    wg_f32, wu_f32, wd_f32, wg_bf, wu_bf, wd_bf, stage, xs_ring, slot_ref, sems, xs_sems):
    b = pl.program_id(0)
    nact = nact_ref[0]
    last = pl.num_programs(0) - 1
    active = b < nact
    e = bexp_ref[b]
    new_expert = (b == 0) | (e != bexp_ref[jnp.maximum(b - 1, 0)])
    blk_rows = MOE_BLOCK * TOKEN_WORD_ROWS

    def xs_fetch(block):
        ring = lax.rem(block, XS_RING)
        return pltpu.make_async_copy(xs_hbm.at[pl.ds(pl.multiple_of(block * blk_rows, SUBLANES), blk_rows), :],
                                     xs_ring.at[ring], xs_sems.at[ring])

    @pl.when(b == 0)
    def _():
        xs_fetch(0).start()

        @pl.when(nact > 1)
        def _():
            xs_fetch(1).start()

    @pl.when(b + 2 < nact)
    def _():
        xs_fetch(b + 2).start()

    def fetch(expert, slot):
        return (pltpu.make_async_copy(wg_hbm.at[expert], wg_f32.at[slot], sems.at[slot, 0]),
                pltpu.make_async_copy(wu_hbm.at[expert], wu_f32.at[slot], sems.at[slot, 1]),
                pltpu.make_async_copy(wd_hbm.at[expert], wd_f32.at[slot], sems.at[slot, 2]))

    def run_end(start, expert):
        return lax.while_loop(lambda j: (j < nact) & (bexp_ref[jnp.minimum(j, last)] == expert),
                              lambda j: j + 1, start)

    def prefetch(block, slot):
        @pl.when(block < nact)
        def _():
            for cp in fetch(bexp_ref[jnp.minimum(block, last)], slot):
                cp.start(priority=WEIGHT_DMA_PRIORITY)

    @pl.when(active & new_expert)
    def _():
        nxt = run_end(b + 1, e)

        @pl.when(b == 0)
        def _():
            slot_ref[0] = 0
            for cp in fetch(e, 0):
                cp.start(priority=WEIGHT_DMA_PRIORITY)
            prefetch(nxt, 1)

        @pl.when(b > 0)
        def _():
            slot_ref[0] = lax.rem(slot_ref[0] + 1, WEIGHT_SLOTS)

        slot = slot_ref[0]
        nxt2 = run_end(nxt + 1, bexp_ref[jnp.minimum(nxt, last)])
        prefetch(jnp.where(nxt < nact, nxt2, nact), lax.rem(slot + 2, WEIGHT_SLOTS))

        cp_g, cp_u, cp_d = fetch(e, slot)
        cp_g.wait()
        wg_bf[...] = wg_f32[slot].astype(BF16)
        cp_u.wait()
        wu_bf[...] = wu_f32[slot].astype(BF16)
        cp_d.wait()
        wd_bf[...] = wd_f32[slot].astype(BF16)

    @pl.when(active)
    def _():
        xs_fetch(b).wait()
        stage[...] = pltpu.bitcast(xs_ring[lax.rem(b, XS_RING)], BF16).astype(F32)
        x = _unpack_rows(stage, 0, MOE_BLOCK).astype(BF16)
        g = jnp.dot(x, wg_bf[...], preferred_element_type=F32)
        u = jnp.dot(x, wu_bf[...], preferred_element_type=F32)
        h = (_silu(g) * u).astype(BF16)
        _pack_rows(jnp.dot(h, wd_bf[...], preferred_element_type=F32), y_ref, MOE_BLOCK)

    @pl.when(jnp.logical_not(active))
    def _():
        y_ref[...] = jnp.zeros(y_ref.shape, y_ref.dtype)


def _experts(xs, w_gate_e, w_up_e, w_down_e, block_exp, nact):
    n_blocks = xs.shape[0] // (MOE_BLOCK * TOKEN_WORD_ROWS)
    _, d, ff = w_gate_e.shape
    hbm = pl.BlockSpec(memory_space=pl.ANY)
    return pl.pallas_call(
        _experts_kernel,
        grid_spec=pltpu.PrefetchScalarGridSpec(
            num_scalar_prefetch=2,
            grid=(n_blocks,),
            in_specs=[hbm, hbm, hbm, hbm],
            out_specs=pl.BlockSpec((MOE_BLOCK * TOKEN_ROWS, LANES), lambda b, bexp, nact: (b, 0)),
            scratch_shapes=[pltpu.VMEM((WEIGHT_SLOTS, d, ff), F32), pltpu.VMEM((WEIGHT_SLOTS, d, ff), F32),
                            pltpu.VMEM((WEIGHT_SLOTS, ff, d), F32),
                            pltpu.VMEM((d, ff), BF16), pltpu.VMEM((d, ff), BF16), pltpu.VMEM((ff, d), BF16),
                            pltpu.VMEM((MOE_BLOCK * TOKEN_ROWS, LANES), F32),
                            pltpu.VMEM((XS_RING, MOE_BLOCK * TOKEN_WORD_ROWS, LANES), jnp.uint32),
                            pltpu.SMEM((1,), jnp.int32), pltpu.SemaphoreType.DMA((WEIGHT_SLOTS, 3)),
                            pltpu.SemaphoreType.DMA((XS_RING,))]),
        out_shape=jax.ShapeDtypeStruct((n_blocks * MOE_BLOCK * TOKEN_ROWS, LANES), F32),
        compiler_params=_cparams("arbitrary"),
        name="experts",
    )(block_exp, nact, xs, w_gate_e, w_up_e, w_down_e)


def _combine_kernel(pstart_ref, idx_ref, rank_ref, x_ref, gw_ref, y_hbm, wgs_ref, wus_ref, wds_ref, g_ref, b_ref,
                    op_ref, os_ref, ybuf, sem, *, n_prompt_tiles):
    i = pl.program_id(0)
    tt = TOKEN_TILE

    def token(t, c):
        for j in range(TOP_K):
            a = t * TOP_K + j
            d = pstart_ref[idx_ref[0, 0, a]] + rank_ref[0, 0, a]
            pltpu.make_async_copy(y_hbm.at[_slot_rows(d), :], ybuf.at[_slot_rows(j * tt + t), :],
                                  sem).start(priority=j % 2)
        return c

    lax.fori_loop(0, tt, token, 0)

    x = x_ref[...]
    xb = x.astype(BF16)
    hs = _silu(jnp.dot(xb, wgs_ref[...], preferred_element_type=F32)) * jnp.dot(
        xb, wus_ref[...], preferred_element_type=F32)
    f = jnp.dot(hs.astype(BF16), wds_ref[...], preferred_element_type=F32)

    for j in range(TOP_K):
        pltpu.make_async_copy(y_hbm.at[_slot_rows(0, tt), :], ybuf.at[_slot_rows(j * tt, tt), :], sem).wait()
    gw = gw_ref[...]
    for j in range(TOP_K):
        f = f + gw[:, j:j + 1] * _unpack_rows(ybuf, j * tt, tt)
    out = _layer_norm(DEEPNORM_ALPHA * x + f, g_ref[...], b_ref[...])

    @pl.when(i < n_prompt_tiles)
    def _():
        op_ref[...] = out

    @pl.when(i >= n_prompt_tiles)
    def _():
        os_ref[...] = out


def _combine(x1, idx3, rank3, pstart, gw, y, wgs, wus, wds, ln_g, ln_b, n_prompt_rows):
    t, d = x1.shape
    tt = TOKEN_TILE
    nt = t // tt
    npt = n_prompt_rows // tt
    full = lambda a: pl.BlockSpec(a.shape, lambda i, *_: (0,) * a.ndim)
    smem_tile = pl.BlockSpec((1, 1, tt * TOP_K), lambda i, *_: (i, 0, 0), memory_space=pltpu.SMEM)
    return pl.pallas_call(
        functools.partial(_combine_kernel, n_prompt_tiles=npt),
        grid_spec=pltpu.PrefetchScalarGridSpec(
            num_scalar_prefetch=1,
            grid=(nt,),
            in_specs=[smem_tile, smem_tile,
                      pl.BlockSpec((tt, d), lambda i, *_: (i, 0)),
                      pl.BlockSpec((tt, LANES), lambda i, *_: (i, 0)),
                      pl.BlockSpec(memory_space=pl.ANY),
                      full(wgs), full(wus), full(wds), full(ln_g), full(ln_b)],
            out_specs=[pl.BlockSpec((tt, d), lambda i, *_: (jnp.minimum(i, npt - 1), 0)),
                       pl.BlockSpec((tt, d), lambda i, *_: (jnp.maximum(i - npt, 0), 0))],
            scratch_shapes=[pltpu.VMEM((TOP_K * tt * TOKEN_ROWS, LANES), F32), pltpu.SemaphoreType.DMA]),
        out_shape=[jax.ShapeDtypeStruct((n_prompt_rows, d), F32),
                   jax.ShapeDtypeStruct((t - n_prompt_rows, d), F32)],
        compiler_params=_cparams("arbitrary"),
        name="combine",
    )(pstart, idx3, rank3, x1, gw, y, wgs, wus, wds, ln_g, ln_b)


def _moe_plan(idx, rank, counts, n_tokens):
    cnt = counts[0].astype(jnp.int32)
    padded = ((cnt + MOE_BLOCK - 1) // MOE_BLOCK) * MOE_BLOCK
    pend = jnp.cumsum(padded)
    pstart = pend - padded
    n_assign = n_tokens * TOP_K
    n_blocks = (n_assign + N_EXPERTS * (MOE_BLOCK - 1)) // MOE_BLOCK
    tile3 = lambda a: a[:, :TOP_K].reshape(n_tokens // TOKEN_TILE, 1, TOKEN_TILE * TOP_K)
    nact = (pend[-1:] // MOE_BLOCK).astype(jnp.int32)
    block_start = jnp.arange(n_blocks, dtype=jnp.int32) * MOE_BLOCK
    block_exp = jnp.minimum(jnp.sum(pend[None, :] <= block_start[:, None], axis=1), N_EXPERTS - 1)
    return tile3(idx), tile3(rank), pstart.astype(jnp.int32), cnt, nact, block_exp.astype(jnp.int32), n_blocks


def kernel(x_prompt, x_sample, cache_k, cache_v, state_conv, w_in, sinks, w_attn_proj, conv_w, conv_b,
           conv_ln_g, conv_ln_b, w_conv_proj, w_out, ln1_g, ln1_b, w_router, b_router, w_gate_e, w_up_e,
           w_down_e, w_gate_s, w_up_s, w_down_s, ln2_g, ln2_b):
    assert w_in.shape[0] == DEPTH
    batch, seq, d = x_prompt.shape
    nbatch = x_sample.shape[0]
    n_prompt = batch * seq
    n_tokens = n_prompt + nbatch
    ch = conv_w.shape[2]
    row = lambda a: a[0].reshape(1, -1)

    w_in_bf = w_in[0].astype(BF16)
    wap = w_attn_proj[0].astype(BF16)
    wcp = w_conv_proj[0].astype(BF16)
    wout = w_out[0].astype(BF16)
    wgs = w_gate_s[0].astype(BF16)
    wus = w_up_s[0].astype(BF16)
    wds = w_down_s[0].astype(BF16)
    wr = w_router[0].astype(F32)
    wr_hi = wr.astype(BF16)
    wr_lo = (wr - wr_hi.astype(F32)).astype(BF16)

    xp2d = x_prompt.reshape(n_prompt, d)
    xs2d = x_sample.reshape(nbatch, d)

    zp = _inproj(xp2d, w_in_bf, tm=1024, tn=1536)
    o_p = _attn_prompt(zp, sinks[0], batch, seq)
    c_p, newconv_p = _conv_prompt(zp, conv_w[0], row(conv_b), row(conv_ln_g), row(conv_ln_b), batch, seq, ts=256)
    x1_p = _merge(xp2d, o_p, c_p, zp, wap, wcp, wout, row(ln1_g), row(ln1_b), tm=256)
    k_off, v_off, ca_off = ATTN_WIDTH, ATTN_WIDTH + KV_WIDTH, ATTN_WIDTH + 2 * KV_WIDTH
    zp3 = zp.reshape(batch, seq, -1)
    newk_p = zp3[:, seq - WINDOW:, k_off:v_off].reshape(1, batch, WINDOW, N_KV_HEADS, HEAD_DIM)
    newv_p = zp3[:, seq - WINDOW:, v_off:ca_off].reshape(1, batch, WINDOW, N_KV_HEADS, HEAD_DIM)

    zs = _inproj(xs2d, w_in_bf, tm=nbatch, tn=1536)
    own_block = (jnp.arange(KV_WIDTH)[None, :] // HEAD_DIM) == (jnp.arange(N_HEADS)[:, None] // GQA_GROUP)
    q3 = jnp.where(own_block[None], jnp.tile(zs[:, :k_off].reshape(nbatch, N_HEADS, HEAD_DIM), (1, 1, N_KV_HEADS)), 0.0)
    kn3 = zs[:, k_off:v_off].reshape(nbatch, 1, KV_WIDTH)
    vn3 = zs[:, v_off:ca_off].reshape(nbatch, 1, KV_WIDTH)
    slopes_col = jnp.asarray(ALIBI_SLOPES, F32).reshape(N_HEADS, 1)
    o_s3, newk_s, newv_s = _attn_sample(
        q3, kn3, vn3, cache_k[0].reshape(nbatch, WINDOW, KV_WIDTH), cache_v[0].reshape(nbatch, WINDOW, KV_WIDTH),
        sinks[0].reshape(N_HEADS, 1), slopes_col, bb=8)
    c_s, newconv_s = _conv_sample(state_conv[0], zs[:, ca_off:ca_off + ch], zs[:, ca_off + ch:ca_off + 2 * ch],
                                  conv_w[0], row(conv_b), row(conv_ln_g), row(conv_ln_b), bb=8)
    x1_s = _merge(xs2d, o_s3.reshape(nbatch, ATTN_WIDTH), c_s, zs, wap, wcp, wout, row(ln1_g), row(ln1_b),
                  tm=nbatch)

    x1, idx, gw, rank, counts = _router(x1_p, x1_s, wr_hi, wr_lo, row(b_router))
    idx3, rank3, pstart, cnt, nact, block_exp, n_blocks = _moe_plan(idx, rank, counts, n_tokens)
    xs_sorted = _dispatch(x1, idx3, rank3, pstart, cnt, nact, n_blocks)
    y_sorted = _experts(xs_sorted, w_gate_e[0], w_up_e[0], w_down_e[0], block_exp, nact)
    y_p, y_s = _combine(x1, idx3, rank3, pstart, gw, y_sorted, wgs, wus, wds, row(ln2_g), row(ln2_b), n_prompt)

    return (y_p.reshape(batch, seq, d), y_s.reshape(nbatch, 1, d),
            newk_p, newv_p, newconv_p.reshape(1, batch, CONV_WIDTH - 1, ch),
            newk_s.reshape(1, nbatch, WINDOW, N_KV_HEADS, HEAD_DIM),
            newv_s.reshape(1, nbatch, WINDOW, N_KV_HEADS, HEAD_DIM),
            newconv_s.reshape(1, nbatch, CONV_WIDTH - 1, ch))
```
